```python
import math
import jax, jax.numpy as jnp
from jax import lax
import numpy as np

D_MODEL = 1024
BATCH = 8
SEQ = 4096
DEPTH = 2

N_MEM = 256
EPS = 1e-6

RET_WIDTH = D_MODEL // 2
RET_HEADS = 4
RET_HEAD_DIM = RET_WIDTH // RET_HEADS
RET_CHUNK = 128
ROPE_BASE = 10000.0
S5_WIDTH = D_MODEL - RET_WIDTH
S5_GROUP = 16
S5_GROUPS = S5_WIDTH // S5_GROUP
S5_STATE = 64
EVEN_IN = 4 * RET_WIDTH + S5_WIDTH

GDN_HEADS = 8
GDN_HEAD_DIM = D_MODEL // GDN_HEADS
GDN_WIDTH = GDN_HEADS * GDN_HEAD_DIM
GDN_CONV = 4
GDN_CHUNK = 64
ODD_IN = 4 * GDN_WIDTH + 2 * GDN_HEADS

XA_HEADS = 4
XA_HEAD_DIM = D_MODEL // XA_HEADS

FFN_DIM = ((8 * D_MODEL) // 3 + 255) // 256 * 256
FFN_CONV = 3

kernel_name = "hybrid_retention_s5_gdn_convffn"

F32 = jnp.float32


def rmsnorm(x, g):
    xf = x.astype(F32)
    y = xf * lax.rsqrt(jnp.mean(xf * xf, axis=-1, keepdims=True) + EPS)
    return (y * g.astype(F32)).astype(x.dtype)


def causal_dwconv(x, w):
    k_w, ch = w.shape
    return lax.conv_general_dilated(x, w[:, None, :].astype(x.dtype), window_strides=(1,),
                                    padding=[(k_w - 1, 0)],
                                    dimension_numbers=('NWC', 'WIO', 'NWC'),
                                    feature_group_count=ch)


def rotary(x, positions):
    half = x.shape[-1] // 2
    inv = jnp.exp(-math.log(ROPE_BASE) * jnp.arange(half, dtype=F32) / half)
    ang = positions.astype(F32)[:, None] * inv[None, :]
    cos = jnp.cos(ang)[None, :, None, :]
    sin = jnp.sin(ang)[None, :, None, :]
    x1, x2 = x[..., :half], x[..., half:]
    return jnp.concatenate([x1 * cos - x2 * sin, x1 * sin + x2 * cos], axis=-1)


def retention_chunkwise(q, k, v):
    b, h, s, dh = q.shape
    c = RET_CHUNK
    n = s // c
    log_gamma = jnp.log1p(-jnp.exp2(-5.0 - jnp.arange(h, dtype=F32)))
    idx = jnp.arange(c, dtype=F32)
    diff = idx[:, None] - idx[None, :]
    causal = diff >= 0
    intra = jnp.where(causal, jnp.exp(log_gamma[:, None, None] * jnp.where(causal, diff, 0.0)), 0.0)
    q = q.reshape(b, h, n, c, dh)
    k = k.reshape(b, h, n, c, dh)
    v = v.reshape(b, h, n, c, dh)
    scores = jnp.einsum('bhnid,bhnjd->bhnij', q, k) * intra[None, :, None]
    inner = jnp.einsum('bhnij,bhnjd->bhnid', scores, v)
    k_dec = k * jnp.exp(log_gamma[:, None] * (c - 1 - idx))[None, :, None, :, None]
    kv = jnp.einsum('bhnjd,bhnje->nbhde', k_dec, v)
    chunk_decay = jnp.exp(log_gamma * c)[None, :, None, None]

    def step(state, kv_n):
        return state * chunk_decay + kv_n, state

    _, prev = lax.scan(step, jnp.zeros((b, h, dh, dh), F32), kv)
    q_dec = q * jnp.exp(log_gamma[:, None] * (idx + 1))[None, :, None, :, None]
    cross = jnp.einsum('bhnid,nbhde->bhnie', q_dec, prev)
    return (inner + cross).reshape(b, h, s, dh)


def complex_affine_combine(e1, e2):
    a1r, a1i, b1r, b1i = e1
    a2r, a2i, b2r, b2i = e2
    return (a2r * a1r - a2i * a1i,
            a2r * a1i + a2i * a1r,
            a2r * b1r - a2i * b1i + b2r,
            a2r * b1i + a2i * b1r + b2i)


def s5_ssm(u, lam_re, lam_im, b_re, b_im, c_re, c_im, d, log_dt):
    bsz, s, _ = u.shape
    uf = u.astype(F32).reshape(bsz, s, S5_GROUPS, S5_GROUP)
    lr, li = lam_re.astype(F32), lam_im.astype(F32)
    dt = jnp.exp(log_dt.astype(F32))[:, None]
    mag = jnp.exp(lr * dt)
    a_re = mag * jnp.cos(li * dt)
    a_im = mag * jnp.sin(li * dt)
    den = lr * lr + li * li
    z_re = ((a_re - 1.0) * lr + a_im * li) / den
    z_im = (a_im * lr - (a_re - 1.0) * li) / den
    br, bi = b_re.astype(F32), b_im.astype(F32)
    bb_re = z_re[:, None, :] * br - z_im[:, None, :] * bi
    bb_im = z_re[:, None, :] * bi + z_im[:, None, :] * br
    bu_re = jnp.einsum('bsgh,ghp->bsgp', uf, bb_re)
    bu_im = jnp.einsum('bsgh,ghp->bsgp', uf, bb_im)
    elems = (jnp.broadcast_to(a_re, bu_re.shape), jnp.broadcast_to(a_im, bu_re.shape), bu_re, bu_im)
    _, _, st_re, st_im = lax.associative_scan(complex_affine_combine, elems, axis=1)
    y = (jnp.einsum('bsgp,gph->bsgh', st_re, c_re.astype(F32))
         - jnp.einsum('bsgp,gph->bsgh', st_im, c_im.astype(F32))
         + d.astype(F32) * uf)
    return y.reshape(bsz, s, S5_WIDTH)


def even_mixer(h, w_in, ret_norm, lam_re, lam_im, b_re, b_im, c_re, c_im, s5_d, s5_log_dt,
               w_glu, b_glu, w_out):
    bsz, s, _ = h.shape
    proj = h @ w_in
    q, k, v, gate, u = jnp.split(proj, [RET_WIDTH, 2 * RET_WIDTH, 3 * RET_WIDTH, 4 * RET_WIDTH], axis=-1)
    pos = jnp.arange(s)

    def heads(t):
        return t.astype(F32).reshape(bsz, s, RET_HEADS, RET_HEAD_DIM)

    qh = rotary(heads(q), pos)
    kh = rotary(heads(k), pos) * (RET_HEAD_DIM ** -0.5)
    o = retention_chunkwise(qh.transpose(0, 2, 1, 3), kh.transpose(0, 2, 1, 3),
                            heads(v).transpose(0, 2, 1, 3)).transpose(0, 2, 1, 3)
    o = o * lax.rsqrt(jnp.mean(o * o, axis=-1, keepdims=True) + EPS)
    o = o.reshape(bsz, s, RET_WIDTH) * ret_norm.astype(F32) * jax.nn.silu(gate.astype(F32))
    y = s5_ssm(u, lam_re, lam_im, b_re, b_im, c_re, c_im, s5_d, s5_log_dt)
    y = jax.nn.gelu(y)
    y = y * jax.nn.sigmoid(y @ w_glu.astype(F32) + b_glu.astype(F32))
    merged = jnp.concatenate([o, y], axis=-1).astype(h.dtype)
    return merged @ w_out


def gated_delta_chunkwise(q, k, v, g, beta):
    b, h, s, dk = q.shape
    dv = v.shape[-1]
    c = GDN_CHUNK
    n = s // c
    q = q.reshape(b, h, n, c, dk)
    k = k.reshape(b, h, n, c, dk)
    v = v.reshape(b, h, n, c, dv)
    gc = jnp.cumsum(g.reshape(b, h, n, c), axis=-1)
    beta = beta.reshape(b, h, n, c)
    kb = k * beta[..., None]
    vb = v * beta[..., None]
    incl = jnp.tril(jnp.ones((c, c), bool))
    strict = jnp.tril(jnp.ones((c, c), bool), -1)
    gdiff = gc[..., :, None] - gc[..., None, :]
    decay = jnp.where(incl, jnp.exp(jnp.where(incl, gdiff, 0.0)), 0.0)
    a_mat = jnp.where(strict, jnp.einsum('bhnid,bhnjd->bhnij', kb, k) * decay, 0.0)
    eye = jnp.eye(c, dtype=F32)
    t_mat = lax.linalg.triangular_solve(a_mat + eye, jnp.broadcast_to(eye, a_mat.shape),
                                        left_side=True, lower=True)
    w = jnp.einsum('bhnij,bhnjd->bhnid', t_mat, kb * jnp.exp(gc)[..., None])
    u = jnp.einsum('bhnij,bhnjd->bhnid', t_mat, vb)
    qk = jnp.where(incl, jnp.einsum('bhnid,bhnjd->bhnij', q, k) * decay, 0.0)
    q_dec = q * jnp.exp(gc)[..., None]
    k_dec = k * jnp.exp(gc[..., -1:] - gc)[..., None]
    g_last = jnp.exp(gc[..., -1])
    xs = tuple(jnp.moveaxis(t, 2, 0) for t in (q_dec, k_dec, u, w, qk, g_last))

    def step(state, inp):
        qd, kd, un, wn, qkn, gl = inp
        v_new = un - jnp.einsum('bhcd,bhde->bhce', wn, state)
        o = jnp.einsum('bhcd,bhde->bhce', qd, state) + jnp.einsum('bhij,bhje->bhie', qkn, v_new)
        state = state * gl[..., None, None] + jnp.einsum('bhcd,bhce->bhde', kd, v_new)
        return state, o

    _, o = lax.scan(step, jnp.zeros((b, h, dk, dv), F32), xs)
    return jnp.moveaxis(o, 0, 2).reshape(b, h, s, dv)


def odd_mixer(h, w_in, conv_w, a_log, dt_bias, o_norm, w_out):
    bsz, s, _ = h.shape
    proj = h @ w_in
    qkv, z, b_in, a_in = jnp.split(proj, [3 * GDN_WIDTH, 4 * GDN_WIDTH, 4 * GDN_WIDTH + GDN_HEADS], axis=-1)
    qkv = jax.nn.silu(causal_dwconv(qkv, conv_w)).astype(F32)
    q, k, v = jnp.split(qkv, 3, axis=-1)

    def heads(t):
        return t.reshape(bsz, s, GDN_HEADS, GDN_HEAD_DIM).transpose(0, 2, 1, 3)

    def l2n(t):
        return t * lax.rsqrt(jnp.sum(t * t, axis=-1, keepdims=True) + EPS)

    q = l2n(heads(q)) * (GDN_HEAD_DIM ** -0.5)
    k = l2n(heads(k))
    v = heads(v)
    beta = jax.nn.sigmoid(b_in.astype(F32)).transpose(0, 2, 1)
    g = -(jnp.exp(a_log.astype(F32)) * jax.nn.softplus(a_in.astype(F32) + dt_bias.astype(F32)))
    g = g.transpose(0, 2, 1)
    o = gated_delta_chunkwise(q, k, v, g, beta).transpose(0, 2, 1, 3)
    o = o * lax.rsqrt(jnp.mean(o * o, axis=-1, keepdims=True) + EPS) * o_norm.astype(F32)
    o = o * jax.nn.silu(z.astype(F32).reshape(bsz, s, GDN_HEADS, GDN_HEAD_DIM))
    return o.reshape(bsz, s, GDN_WIDTH).astype(h.dtype) @ w_out


def memory_cross_attention(h, mem_n, wq, wkv, wo):
    bsz, s, _ = h.shape
    m = mem_n.shape[1]
    q = (h @ wq).reshape(bsz, s, XA_HEADS, XA_HEAD_DIM)
    k, v = jnp.split(mem_n @ wkv, 2, axis=-1)
    k = k.reshape(bsz, m, XA_HEADS, XA_HEAD_DIM)
    v = v.reshape(bsz, m, XA_HEADS, XA_HEAD_DIM)
    scores = jnp.einsum('bshd,bmhd->bhsm', q, k).astype(F32) * (XA_HEAD_DIM ** -0.5)
    p = jax.nn.softmax(scores, axis=-1).astype(h.dtype)
    o = jnp.einsum('bhsm,bmhd->bshd', p, v).reshape(bsz, s, D_MODEL)
    return o @ wo


def conv_ffn(h, w_up, conv_w, w_down):
    hu = causal_dwconv(h @ w_up, conv_w)
    up, gate = jnp.split(hu, 2, axis=-1)
    return (jax.nn.silu(gate) * up) @ w_down


def setup_inputs(seed: int = 0) -> dict:
    key = jax.random.key(seed)
    it = iter(jax.random.split(key, 64))

    def nrm(shape, scale):
        return jax.random.normal(next(it), shape, F32) * scale

    def dense(fan_in, fan_out):
        return nrm((fan_in, fan_out), fan_in ** -0.5)

    def gain(n):
        return 1.0 + nrm((n,), 0.02)

    def log_uniform(shape, lo, hi):
        return jax.random.uniform(next(it), shape, F32, math.log(lo), math.log(hi))

    def common(p):
        return {
            p + "xa_norm": gain(D_MODEL),
            p + "mem_norm": gain(D_MODEL),
            p + "xa_wq": dense(D_MODEL, D_MODEL),
            p + "xa_wkv": dense(D_MODEL, 2 * D_MODEL),
            p + "xa_wo": dense(D_MODEL, D_MODEL),
            p + "ffn_norm": gain(D_MODEL),
            p + "ffn_w_up": dense(D_MODEL, 2 * FFN_DIM),
            p + "ffn_conv": nrm((FFN_CONV, 2 * FFN_DIM), FFN_CONV ** -0.5),
            p + "ffn_w_down": dense(FFN_DIM, D_MODEL),
        }

    out = {
        "x": nrm((BATCH, SEQ, D_MODEL), 1.0),
        "mem": nrm((BATCH, N_MEM, D_MODEL), 1.0),
        "l0_mix_norm": gain(D_MODEL),
        "l0_w_in": dense(D_MODEL, EVEN_IN),
        "l0_ret_norm": gain(RET_WIDTH),
        "l0_s5_lambda_re": -0.5 + nrm((S5_GROUPS, S5_STATE), 0.01),
        "l0_s5_lambda_im": math.pi * jnp.broadcast_to(jnp.arange(S5_STATE, dtype=F32), (S5_GROUPS, S5_STATE))
                           + nrm((S5_GROUPS, S5_STATE), 0.01),
        "l0_s5_b_re": nrm((S5_GROUPS, S5_GROUP, S5_STATE), (2 * S5_GROUP) ** -0.5),
        "l0_s5_b_im": nrm((S5_GROUPS, S5_GROUP, S5_STATE), (2 * S5_GROUP) ** -0.5),
        "l0_s5_c_re": nrm((S5_GROUPS, S5_STATE, S5_GROUP), (2 * S5_STATE) ** -0.5),
        "l0_s5_c_im": nrm((S5_GROUPS, S5_STATE, S5_GROUP), (2 * S5_STATE) ** -0.5),
        "l0_s5_d": nrm((S5_GROUPS, S5_GROUP), 1.0),
        "l0_s5_log_dt": log_uniform((S5_GROUPS,), 1e-3, 1e-1),
        "l0_s5_w_glu": dense(S5_WIDTH, S5_WIDTH),
        "l0_s5_b_glu": nrm((S5_WIDTH,), 0.01),
        "l0_w_out": dense(D_MODEL, D_MODEL),
    }
    out.update(common("l0_"))
    dt = jnp.exp(log_uniform((GDN_HEADS,), 1e-3, 1e-1))
    out.update({
        "l1_mix_norm": gain(D_MODEL),
        "l1_w_in": dense(D_MODEL, ODD_IN),
        "l1_conv": nrm((GDN_CONV, 3 * GDN_WIDTH), GDN_CONV ** -0.5),
        "l1_a_log": jnp.log(jax.random.uniform(next(it), (GDN_HEADS,), F32, 1.0, 16.0)),
        "l1_dt_bias": dt + jnp.log(-jnp.expm1(-dt)),
        "l1_o_norm": gain(GDN_HEAD_DIM),
        "l1_w_out": dense(GDN_WIDTH, D_MODEL),
    })
    out.update(common("l1_"))
    out["final_norm"] = gain(D_MODEL)
    return out


def reference(x, mem,
              l0_mix_norm, l0_w_in, l0_ret_norm, l0_s5_lambda_re, l0_s5_lambda_im, l0_s5_b_re, l0_s5_b_im,
              l0_s5_c_re, l0_s5_c_im, l0_s5_d, l0_s5_log_dt, l0_s5_w_glu, l0_s5_b_glu, l0_w_out,
              l0_xa_norm, l0_mem_norm, l0_xa_wq, l0_xa_wkv, l0_xa_wo,
              l0_ffn_norm, l0_ffn_w_up, l0_ffn_conv, l0_ffn_w_down,
              l1_mix_norm, l1_w_in, l1_conv, l1_a_log, l1_dt_bias, l1_o_norm, l1_w_out,
              l1_xa_norm, l1_mem_norm, l1_xa_wq, l1_xa_wkv, l1_xa_wo,
              l1_ffn_norm, l1_ffn_w_up, l1_ffn_conv, l1_ffn_w_down,
              final_norm):
    mixers = (
        lambda h: even_mixer(h, l0_w_in, l0_ret_norm, l0_s5_lambda_re, l0_s5_lambda_im, l0_s5_b_re,
                             l0_s5_b_im, l0_s5_c_re, l0_s5_c_im, l0_s5_d, l0_s5_log_dt,
                             l0_s5_w_glu, l0_s5_b_glu, l0_w_out),
        lambda h: odd_mixer(h, l1_w_in, l1_conv, l1_a_log, l1_dt_bias, l1_o_norm, l1_w_out),
    )
    commons = (
        (l0_mix_norm, l0_xa_norm, l0_mem_norm, l0_xa_wq, l0_xa_wkv, l0_xa_wo,
         l0_ffn_norm, l0_ffn_w_up, l0_ffn_conv, l0_ffn_w_down),
        (l1_mix_norm, l1_xa_norm, l1_mem_norm, l1_xa_wq, l1_xa_wkv, l1_xa_wo,
         l1_ffn_norm, l1_ffn_w_up, l1_ffn_conv, l1_ffn_w_down),
    )
    for i in range(DEPTH):
        (mix_norm, xa_norm, mem_norm, xa_wq, xa_wkv, xa_wo,
         ffn_norm, ffn_w_up, ffn_conv, ffn_w_down) = commons[i]
        x = x + mixers[i](rmsnorm(x, mix_norm))
        x = x + memory_cross_attention(rmsnorm(x, xa_norm), rmsnorm(mem, mem_norm), xa_wq, xa_wkv, xa_wo)
        x = x + conv_ffn(rmsnorm(x, ffn_norm), ffn_w_up, ffn_conv, ffn_w_down)
    return rmsnorm(x, final_norm)
```

```python
import functools
import math

import jax
import jax.numpy as jnp
from jax import lax
from jax.experimental import pallas as pl
from jax.experimental.pallas import tpu as pltpu

F32 = jnp.float32
BF16 = jnp.bfloat16
EPS = 1e-6

V7X_VMEM_BYTES = 64 * 1024 * 1024
VMEM_LIMIT_BYTES = V7X_VMEM_BYTES - 8 * 1024 * 1024
LANES = 128
SUBLANES = 8

RET_HEADS = 4
RET_HEAD_DIM = 128
RET_CHUNK = 128
ROPE_BASE = 10000.0
S5_GROUP = 16
S5_STATE = 64
S5_CHUNK = 64
GDN_HEADS = 8
GDN_HEAD_DIM = 128
GDN_CONV = 4
GDN_CHUNK = 64
XA_HEADS = 4
FFN_CONV = 3
FFN_COLS = 256

HIGHEST = lax.Precision.HIGHEST


def _params(*semantics):
    return pltpu.CompilerParams(dimension_semantics=semantics, vmem_limit_bytes=VMEM_LIMIT_BYTES)


def _resident(shape):
    nd = len(shape)
    return pl.BlockSpec(shape, lambda *_: (0,) * nd)


def _dot(a, b, precision=None):
    return jnp.dot(a, b, preferred_element_type=F32, precision=precision)


def _dot_nt(a, b):
    return lax.dot_general(a, b, (((1,), (1,)), ((), ())), preferred_element_type=F32)


def _dot_tn(a, b):
    return lax.dot_general(a, b, (((0,), (0,)), ((), ())), preferred_element_type=F32)


def _rms(x, g):
    return x * lax.rsqrt(jnp.mean(x * x, axis=-1, keepdims=True) + EPS) * g


def _silu(x):
    return x * jax.nn.sigmoid(x)


def _norm_proj_kernel(x_ref, g_ref, *refs):
    n_out = len(refs) // 2
    xn = _rms(x_ref[...], g_ref[...]).astype(BF16)
    for w_ref, o_ref in zip(refs[:n_out], refs[n_out:]):
        o_ref[...] = _dot(xn, w_ref[...]).astype(o_ref.dtype)


def norm_proj(x, gain, weights, out_dtypes, tm):
    t, d = x.shape
    in_specs = [pl.BlockSpec((tm, d), lambda i: (i, 0)), _resident((1, d))]
    in_specs += [_resident(w.shape) for w in weights]
    out_specs = [pl.BlockSpec((tm, w.shape[1]), lambda i: (i, 0)) for w in weights]
    out_shape = [jax.ShapeDtypeStruct((t, w.shape[1]), dt) for w, dt in zip(weights, out_dtypes)]
    return pl.pallas_call(
        _norm_proj_kernel, grid=(t // tm,), in_specs=in_specs, out_specs=out_specs,
        out_shape=out_shape, compiler_params=_params("parallel"), name="norm_proj",
    )(x, gain.reshape(1, d), *weights)


def _retention_kernel(q_ref, k_ref, v_ref, gt_ref, cos_ref, sin_ref, intra_ref, qdec_ref, kdec_ref,
                      cdec_ref, rn_ref, o_ref, state_ref, *, n_chunks):
    c = RET_CHUNK

    @pl.when(pl.program_id(1) == 0)
    def _():
        state_ref[...] = jnp.zeros_like(state_ref)

    scale = RET_HEAD_DIM ** -0.5
    for ci in range(n_chunks):
        rows = pl.ds(ci * c, c)
        cos = cos_ref[rows, :]
        sin = sin_ref[rows, :]
        for h in range(RET_HEADS):
            cols = pl.ds(h * RET_HEAD_DIM, RET_HEAD_DIM)
            qh = q_ref[rows, cols]
            kh = k_ref[rows, cols]
            vh = v_ref[rows, cols].astype(BF16)
            qr = qh * cos + pltpu.roll(qh, RET_HEAD_DIM // 2, 1) * sin
            kr = (kh * cos + pltpu.roll(kh, RET_HEAD_DIM // 2, 1) * sin) * scale
            scores = _dot_nt(qr.astype(BF16), kr.astype(BF16)) * intra_ref[h]
            inner = _dot(scores.astype(BF16), vh)
            st = state_ref[h]
            cross = _dot((qr * qdec_ref[:, cols]).astype(BF16), st.astype(BF16))
            kv = _dot_tn((kr * kdec_ref[:, cols]).astype(BF16), vh)
            state_ref[h] = st * cdec_ref[:, cols] + kv
            o = inner + cross
            o = o * lax.rsqrt(jnp.mean(o * o, axis=-1, keepdims=True) + EPS)
            o_ref[rows, cols] = o * rn_ref[:, cols] * _silu(gt_ref[rows, cols])


def retention(proj, ret_norm, batch, seq, ts):
    t = proj.shape[0]
    width = RET_HEADS * RET_HEAD_DIM
    c = RET_CHUNK
    half = RET_HEAD_DIM // 2
    ns = seq // ts
    inv = jnp.exp(-math.log(ROPE_BASE) * jnp.arange(half, dtype=F32) / half)
    ang = jnp.arange(seq, dtype=F32)[:, None] * inv[None, :]
    cos = jnp.concatenate([jnp.cos(ang), jnp.cos(ang)], axis=-1)
    sin = jnp.concatenate([-jnp.sin(ang), jnp.sin(ang)], axis=-1)
    log_gamma = jnp.log1p(-jnp.exp2(-5.0 - jnp.arange(RET_HEADS, dtype=F32)))
    idx = jnp.arange(c, dtype=F32)
    diff = idx[:, None] - idx[None, :]
    causal = diff >= 0
    intra = jnp.where(causal, jnp.exp(log_gamma[:, None, None] * jnp.where(causal, diff, 0.0)), 0.0)

    def per_head_cols(tab):
        return jnp.repeat(tab.T, RET_HEAD_DIM, axis=1)

    qdec = per_head_cols(jnp.exp(log_gamma[:, None] * (idx + 1)))
    kdec = per_head_cols(jnp.exp(log_gamma[:, None] * (c - 1 - idx)))
    cdec = per_head_cols(jnp.exp(log_gamma * c)[:, None])

    def col_block(j):
        return pl.BlockSpec((ts, width), lambda b, s: (b * ns + s, j))

    in_specs = [col_block(0), col_block(1), col_block(2), col_block(3),
                pl.BlockSpec((ts, RET_HEAD_DIM), lambda b, s: (s, 0)),
                pl.BlockSpec((ts, RET_HEAD_DIM), lambda b, s: (s, 0)),
                _resident(intra.shape), _resident(qdec.shape), _resident(kdec.shape),
                _resident(cdec.shape), _resident((1, width))]
    return pl.pallas_call(
        functools.partial(_retention_kernel, n_chunks=ts // c),
        grid=(batch, ns), in_specs=in_specs,
        out_specs=pl.BlockSpec((ts, width), lambda b, s: (b * ns + s, 0)),
        out_shape=jax.ShapeDtypeStruct((t, width), F32),
        scratch_shapes=[pltpu.VMEM((RET_HEADS, RET_HEAD_DIM, RET_HEAD_DIM), F32)],
        compiler_params=_params("parallel", "arbitrary"), name="retention",
    )(proj, proj, proj, proj, cos, sin, intra, qdec, kdec, cdec, ret_norm.reshape(1, width))


def _s5_prep_kernel(lr_ref, li_ref, ldt_ref, lrc_ref, lic_ref, ldtc_ref, btr_ref, bti_ref, ctr_ref,
                    cti_ref, ktoe_ref, bdr_ref, bdi_ref, cdr_ref, cdi_ref, al_ref, *, chunk):
    hg = S5_GROUP
    n = chunk * hg
    lr, li = lr_ref[...], li_ref[...]
    dt = jnp.exp(ldt_ref[...])
    mag = jnp.exp(lr * dt)
    a_re = mag * jnp.cos(li * dt)
    a_im = mag * jnp.sin(li * dt)
    den = lr * lr + li * li
    z_re = ((a_re - 1.0) * lr + a_im * li) / den
    z_im = (a_im * lr - (a_re - 1.0) * li) / den
    btr, bti = btr_ref[...], bti_ref[...]
    bb_re = z_re * btr - z_im * bti
    bb_im = z_re * bti + z_im * btr
    j = (lax.broadcasted_iota(jnp.int32, (n, S5_STATE), 0) // hg).astype(F32)
    e = (chunk - 1.0) - j
    pm = jnp.exp(lr * dt * e)
    pr = pm * jnp.cos(li * dt * e)
    pi = pm * jnp.sin(li * dt * e)
    bdr_ref[...] = (bb_re * pr - bb_im * pi).astype(bdr_ref.dtype)
    bdi_ref[...] = (bb_re * pi + bb_im * pr).astype(bdi_ref.dtype)
    lm = jnp.exp(lr * dt * chunk)
    al_ref[0:1, :] = lm * jnp.cos(li * dt * chunk)
    al_ref[1:2, :] = lm * jnp.sin(li * dt * chunk)

    lrc, lic = lrc_ref[...], lic_ref[...]
    dtc = jnp.exp(ldtc_ref[...])
    tau = (lax.broadcasted_iota(jnp.int32, (S5_STATE, n), 1) // hg).astype(F32)
    vm = jnp.exp(lrc * dtc * tau)
    vc = vm * jnp.cos(lic * dtc * tau)
    vs = vm * jnp.sin(lic * dtc * tau)
    ctr, cti = ctr_ref[...], cti_ref[...]
    v_re = vc * ctr - vs * cti
    v_im = vc * cti + vs * ctr
    magc = jnp.exp(lrc * dtc)
    ac_re = magc * jnp.cos(lic * dtc)
    ac_im = magc * jnp.sin(lic * dtc)
    cdr_ref[...] = (v_re * ac_re - v_im * ac_im).astype(cdr_ref.dtype)
    cdi_ref[...] = (-(v_re * ac_im + v_im * ac_re)).astype(cdi_ref.dtype)

    p = (_dot(bb_re[0:hg, :], v_re, HIGHEST) - _dot(bb_im[0:hg, :], v_im, HIGHEST))
    lane = lax.broadcasted_iota(jnp.int32, (hg, n), 1)
    for jj in range(chunk):
        if jj == 0:
            blk = p
        else:
            blk = jnp.where(lane >= jj * hg, pltpu.roll(p, jj * hg, 1), 0.0)
        ktoe_ref[jj * hg:(jj + 1) * hg, :] = blk.astype(ktoe_ref.dtype)


def _s5_main_kernel(u_ref, ktoe_ref, bdr_ref, bdi_ref, cdr_ref, cdi_ref, al_ref, d_ref, y_ref,
                    sre_ref, sim_ref, *, batch, n_chunks):
    u = u_ref[...]
    ub = u.astype(BF16)
    e_re = _dot(ub, bdr_ref[...])
    e_im = _dot(ub, bdi_ref[...])
    a_re = al_ref[0:1, :]
    a_im = al_ref[1:2, :]
    sre_ref[0:batch, :] = jnp.zeros((batch, S5_STATE), F32)
    sim_ref[0:batch, :] = jnp.zeros((batch, S5_STATE), F32)
    sre_ref[batch:, :] = e_re[:-batch, :]
    sim_ref[batch:, :] = e_im[:-batch, :]

    def step(nn, carry):
        prev = pl.ds(pl.multiple_of((nn - 1) * batch, batch), batch)
        cur = pl.ds(pl.multiple_of(nn * batch, batch), batch)
        pr, pi = sre_ref[prev, :], sim_ref[prev, :]
        sre_ref[cur, :] = sre_ref[cur, :] + a_re * pr - a_im * pi
        sim_ref[cur, :] = sim_ref[cur, :] + a_re * pi + a_im * pr
        return carry

    lax.fori_loop(1, n_chunks, step, 0)
    y = _dot(ub, ktoe_ref[...])
    y += _dot(sre_ref[...].astype(BF16), cdr_ref[...])
    y += _dot(sim_ref[...].astype(BF16), cdi_ref[...])
    y_ref[...] = y + d_ref[...] * u


def s5_ssm(u, lam_re, lam_im, b_re, b_im, c_re, c_im, d, log_dt, batch, seq):
    t, width = u.shape
    g, hg, p, chunk = width // S5_GROUP, S5_GROUP, S5_STATE, S5_CHUNK
    n = chunk * hg
    nchunks = seq // chunk
    rows = nchunks * batch

    def grp(shape):
        return pl.BlockSpec((None,) + shape, lambda i: (i, 0, 0))

    ldt = jnp.broadcast_to(log_dt[:, None, None], (g, 1, p))
    prep_in = [lam_re[:, None, :], lam_im[:, None, :], ldt,
               lam_re[:, :, None], lam_im[:, :, None], jnp.swapaxes(ldt, 1, 2),
               jnp.tile(b_re, (1, chunk, 1)), jnp.tile(b_im, (1, chunk, 1)),
               jnp.tile(c_re, (1, 1, chunk)), jnp.tile(c_im, (1, 1, chunk))]
    prep_specs = [grp((1, p))] * 3 + [grp((p, 1))] * 3 + [grp((n, p))] * 2 + [grp((p, n))] * 2
    ktoe, bdr, bdi, cdr, cdi, al = pl.pallas_call(
        functools.partial(_s5_prep_kernel, chunk=chunk),
        grid=(g,), in_specs=prep_specs,
        out_specs=[grp((n, n)), grp((n, p)), grp((n, p)), grp((p, n)), grp((p, n)), grp((2, p))],
        out_shape=[jax.ShapeDtypeStruct((g, n, n), BF16),
                   jax.ShapeDtypeStruct((g, n, p), BF16), jax.ShapeDtypeStruct((g, n, p), BF16),
                   jax.ShapeDtypeStruct((g, p, n), BF16), jax.ShapeDtypeStruct((g, p, n), BF16),
                   jax.ShapeDtypeStruct((g, 2, p), F32)],
        compiler_params=_params("parallel"), name="s5_prep",
    )(*prep_in)

    ug = u.reshape(batch, nchunks, chunk, g, hg).transpose(3, 1, 0, 2, 4).reshape(g, rows, n)
    dvec = jnp.tile(d[:, None, :], (1, 1, chunk))
    yg = pl.pallas_call(
        functools.partial(_s5_main_kernel, batch=batch, n_chunks=nchunks),
        grid=(g,),
        in_specs=[grp((rows, n)), grp((n, n)), grp((n, p)), grp((n, p)), grp((p, n)), grp((p, n)),
                  grp((2, p)), grp((1, n))],
        out_specs=grp((rows, n)),
        out_shape=jax.ShapeDtypeStruct((g, rows, n), F32),
        scratch_shapes=[pltpu.VMEM((rows, p), F32), pltpu.VMEM((rows, p), F32)],
        compiler_params=_params("parallel"), name="s5_main",
    )(ug, ktoe, bdr, bdi, cdr, cdi, al, dvec)
    return yg.reshape(g, nchunks, batch, chunk, hg).transpose(2, 1, 3, 0, 4).reshape(t, width)


def _mix0_out_kernel(o_ref, y_ref, x_ref, wglu_ref, bglu_ref, wout_ref, out_ref):
    y = y_ref[...]
    y = 0.5 * y * (1.0 + jnp.tanh(math.sqrt(2.0 / math.pi) * (y + 0.044715 * (y * y * y))))
    y = y * jax.nn.sigmoid(_dot(y.astype(BF16), wglu_ref[...]) + bglu_ref[...])
    merged = jnp.concatenate([o_ref[...].astype(BF16), y.astype(BF16)], axis=-1)
    out_ref[...] = x_ref[...] + _dot(merged, wout_ref[...])


def mix0_out(o, y, x, w_glu, b_glu, w_out, tm):
    t, d = x.shape
    half = o.shape[1]
    row = lambda w: pl.BlockSpec((tm, w), lambda i: (i, 0))
    return pl.pallas_call(
        _mix0_out_kernel, grid=(t // tm,),
        in_specs=[row(half), row(half), row(d), _resident(w_glu.shape), _resident((1, half)),
                  _resident(w_out.shape)],
        out_specs=row(d), out_shape=jax.ShapeDtypeStruct((t, d), F32),
        compiler_params=_params("parallel"), name="mix0_out",
    )(o, y, x, w_glu, b_glu.reshape(1, half), w_out)


def _xattn_kernel(x_ref, g_ref, wq_ref, kv_ref, wo_ref, o_ref):
    x = x_ref[...]
    d = x.shape[1]
    dh = d // XA_HEADS
    xn = _rms(x, g_ref[...]).astype(BF16)
    q = _dot(xn, wq_ref[...]).astype(BF16)
    heads = []
    for h in range(XA_HEADS):
        kh = kv_ref[:, pl.ds(h * dh, dh)]
        vh = kv_ref[:, pl.ds(d + h * dh, dh)]
        s = _dot_nt(q[:, h * dh:(h + 1) * dh], kh) * (dh ** -0.5)
        e = jnp.exp(s - jnp.max(s, axis=-1, keepdims=True))
        p = e * (1.0 / jnp.sum(e, axis=-1, keepdims=True))
        heads.append(_dot(p.astype(BF16), vh).astype(BF16))
    o_ref[...] = x + _dot(jnp.concatenate(heads, axis=-1), wo_ref[...])


def xattn(x, kv, gain, wq, wo, batch, seq, tm):
    t, d = x.shape
    n_mem = kv.shape[0] // batch
    ns = seq // tm
    row = pl.BlockSpec((tm, d), lambda b, s: (b * ns + s, 0))
    return pl.pallas_call(
        _xattn_kernel, grid=(batch, ns),
        in_specs=[row, _resident((1, d)), _resident(wq.shape),
                  pl.BlockSpec((n_mem, 2 * d), lambda b, s: (b, 0)), _resident(wo.shape)],
        out_specs=row, out_shape=jax.ShapeDtypeStruct((t, d), F32),
        compiler_params=_params("parallel", "parallel"), name="xattn",
    )(x, gain.reshape(1, d), wq, kv, wo)


def _ffn_kernel(x_ref, g_ref, wu_ref, wg_ref, cu_ref, cg_ref, wd_ref, fg_ref, o_ref,
                xn_ref, hu_ref, hg_ref, tail_u_ref, tail_g_ref, acc_ref, *, final_norm):
    tm = x_ref.shape[0]
    n_col_blocks = wu_ref.shape[0]
    pad = SUBLANES

    @pl.when(pl.program_id(1) == 0)
    def _():
        tail_u_ref[...] = jnp.zeros_like(tail_u_ref)
        tail_g_ref[...] = jnp.zeros_like(tail_g_ref)

    xn_ref[...] = _rms(x_ref[...], g_ref[...]).astype(BF16)
    acc_ref[...] = jnp.zeros_like(acc_ref)

    def conv(h_ref, tail_ref, w_ref, cb, cur):
        h_ref[0:pad, :] = tail_ref[cb]
        h_ref[pad:pad + tm, :] = cur
        tail_ref[cb] = cur[tm - pad:tm, :]
        w = w_ref[cb]
        return (w[0:1, :] * h_ref[pl.ds(pad - 2, tm), :] + w[1:2, :] * h_ref[pl.ds(pad - 1, tm), :]
                + w[2:3, :] * cur)

    def col_block(cb, carry):
        xn = xn_ref[...]
        up = conv(hu_ref, tail_u_ref, cu_ref, cb, _dot(xn, wu_ref[cb]))
        gate = conv(hg_ref, tail_g_ref, cg_ref, cb, _dot(xn, wg_ref[cb]))
        act = (_silu(gate) * up).astype(BF16)
        acc_ref[...] += _dot(act, wd_ref[cb])
        return carry

    lax.fori_loop(0, n_col_blocks, col_block, 0)
    y = x_ref[...] + acc_ref[...]
    if final_norm:
        y = _rms(y, fg_ref[...])
    o_ref[...] = y


def conv_ffn(x, gain, w_up, conv_w, w_down, final_gain, batch, seq, tm):
    t, d = x.shape
    f = w_down.shape[0]
    fc = FFN_COLS
    nb = f // fc
    ns = seq // tm

    def col_blocks(w):
        return w.reshape(w.shape[0], nb, fc).transpose(1, 0, 2)

    wu, wg = col_blocks(w_up[:, :f]), col_blocks(w_up[:, f:])
    cu, cg = col_blocks(conv_w[:, :f]), col_blocks(conv_w[:, f:])
    wd = w_down.reshape(nb, fc, d)
    final_norm = final_gain is not None
    fg = (final_gain if final_norm else jnp.ones((d,), F32)).reshape(1, d)
    row = pl.BlockSpec((tm, d), lambda b, s: (b * ns + s, 0))
    return pl.pallas_call(
        functools.partial(_ffn_kernel, final_norm=final_norm), grid=(batch, ns),
        in_specs=[row, _resident((1, d)), _resident(wu.shape), _resident(wg.shape),
                  _resident(cu.shape), _resident(cg.shape), _resident(wd.shape), _resident((1, d))],
        out_specs=row, out_shape=jax.ShapeDtypeStruct((t, d), F32),
        scratch_shapes=[pltpu.VMEM((tm, d), BF16),
                        pltpu.VMEM((tm + SUBLANES, fc), F32), pltpu.VMEM((tm + SUBLANES, fc), F32),
                        pltpu.VMEM((nb, SUBLANES, fc), F32), pltpu.VMEM((nb, SUBLANES, fc), F32),
                        pltpu.VMEM((tm, d), F32)],
        compiler_params=_params("parallel", "arbitrary"), name="conv_ffn",
    )(x, gain.reshape(1, d), wu, wg, cu, cg, wd, fg)


def _unit_lower_inverse(a, eye):
    p = -a
    m = eye + p
    for _ in range(5):
        pb = p.astype(BF16)
        p = _dot(pb, pb)
        m = m + _dot(m.astype(BF16), p.astype(BF16))
    return m


def _gdn_kernel(qkv_ref, halo_ref, z_ref, b_ref, a_ref, x_ref, cw_ref, alog_ref, dtb_ref, on_ref,
                wout_ref, o_ref, xs_ref, q_ref, k_ref, v_ref, gc_ref, gct_ref, beta_ref, ob_ref,
                state_ref):
    ts = qkv_ref.shape[0]
    width = GDN_HEADS * GDN_HEAD_DIM
    dh = GDN_HEAD_DIM
    c = GDN_CHUNK
    pad = SUBLANES
    first = pl.program_id(1) == 0

    @pl.when(first)
    def _():
        state_ref[...] = jnp.zeros_like(state_ref)

    xs_ref[0:pad, :] = halo_ref[...]
    xs_ref[pad:pad + ts, :] = qkv_ref[...]

    @pl.when(first)
    def _():
        xs_ref[0:pad, :] = jnp.zeros((pad, 3 * width), F32)
    for blk in range(3 * GDN_HEADS):
        cols = pl.ds(blk * dh, dh)
        y = cw_ref[GDN_CONV - 1:GDN_CONV, cols] * xs_ref[pl.ds(pad, ts), cols]
        for i in range(GDN_CONV - 1):
            y += cw_ref[i:i + 1, cols] * xs_ref[pl.ds(pad - (GDN_CONV - 1) + i, ts), cols]
        y = _silu(y)
        hcols = pl.ds((blk % GDN_HEADS) * dh, dh)
        if blk < GDN_HEADS:
            q_ref[:, hcols] = y * lax.rsqrt(jnp.sum(y * y, axis=-1, keepdims=True) + EPS) * (dh ** -0.5)
        elif blk < 2 * GDN_HEADS:
            k_ref[:, hcols] = y * lax.rsqrt(jnp.sum(y * y, axis=-1, keepdims=True) + EPS)
        else:
            v_ref[:, hcols] = y

    beta_ref[...] = jax.nn.sigmoid(b_ref[...])
    ag = a_ref[...] + dtb_ref[...]
    softplus = jnp.maximum(ag, 0.0) + jnp.log1p(jnp.exp(-jnp.abs(ag)))
    gc = -(jnp.exp(alog_ref[...]) * softplus)
    pos = lax.broadcasted_iota(jnp.int32, (ts, LANES), 0) % c
    shift = 1
    while shift < c:
        gc = gc + jnp.where(pos >= shift, pltpu.roll(gc, shift, 0), 0.0)
        shift *= 2
    gc_ref[...] = gc
    gct_ref[...] = gc.T

    ri = lax.broadcasted_iota(jnp.int32, (2 * c, 2 * c), 0)
    ci = lax.broadcasted_iota(jnp.int32, (2 * c, 2 * c), 1)
    same = (ri // c) == (ci // c)
    incl = same & (ri >= ci)
    strict = same & (ri > ci)
    eye = jnp.where(ri == ci, 1.0, 0.0).astype(F32)

    def pair(dc, carry):
        r0 = pl.multiple_of(dc * 2 * c, 2 * c)
        rows = pl.ds(r0, 2 * c)
        gcd = gc_ref[rows, :]
        gctd = gct_ref[:, rows]
        bet = beta_ref[rows, :]
        for h in range(GDN_HEADS):
            hcols = pl.ds(h * dh, dh)
            q = q_ref[rows, hcols]
            k = k_ref[rows, hcols]
            v = v_ref[rows, hcols]
            gcol = gcd[:, h:h + 1]
            grow = gctd[h:h + 1, :]
            bcol = bet[:, h:h + 1]
            decay = jnp.where(incl, jnp.exp(jnp.where(incl, gcol - grow, 0.0)), 0.0)
            kb = k * bcol
            kbf = k.astype(BF16)
            a_mat = jnp.where(strict, _dot_nt(kb.astype(BF16), kbf) * decay, 0.0)
            t_mat = _unit_lower_inverse(a_mat, eye).astype(BF16)
            eg = jnp.exp(gcol)
            w = _dot(t_mat, (kb * eg).astype(BF16)).astype(BF16)
            u = _dot(t_mat, (v * bcol).astype(BF16))
            qk = jnp.where(incl, _dot_nt(q.astype(BF16), kbf) * decay, 0.0)
            qd = (q * eg).astype(BF16)
            gl0 = gcd[c - 1:c, h:h + 1]
            gl1 = gcd[2 * c - 1:2 * c, h:h + 1]
            glcol = jnp.concatenate([jnp.broadcast_to(gl0, (c, 1)), jnp.broadcast_to(gl1, (c, 1))], axis=0)
            kd = (k * jnp.exp(glcol - gcol)).astype(BF16)
            st = state_ref[h]
            sb = st.astype(BF16)
            vn0 = u[0:c] - _dot(w[0:c], sb)
            o0 = _dot(qd[0:c], sb)
            st = st * jnp.exp(gl0) + _dot_tn(kd[0:c], vn0.astype(BF16))
            sb = st.astype(BF16)
            vn1 = u[c:] - _dot(w[c:], sb)
            o1 = _dot(qd[c:], sb)
            state_ref[h] = st * jnp.exp(gl1) + _dot_tn(kd[c:], vn1.astype(BF16))
            vn = jnp.concatenate([vn0, vn1], axis=0).astype(BF16)
            o = jnp.concatenate([o0, o1], axis=0) + _dot(qk.astype(BF16), vn)
            o = o * lax.rsqrt(jnp.mean(o * o, axis=-1, keepdims=True) + EPS) * on_ref[...]
            ob_ref[rows, hcols] = (o * _silu(z_ref[rows, hcols])).astype(BF16)
        return carry

    lax.fori_loop(0, ts // (2 * c), pair, 0)
    o_ref[...] = x_ref[...] + _dot(ob_ref[...], wout_ref[...])


def gated_deltanet(proj, gates, x, conv_w, a_log, dt_bias, o_norm, w_out, batch, seq, ts):
    t, d = x.shape
    width = GDN_HEADS * GDN_HEAD_DIM
    ns = seq // ts
    tiles_per_halo = ts // SUBLANES

    def lane_pad(vec):
        return jnp.pad(vec, (0, LANES - vec.shape[0])).reshape(1, LANES)

    def row(w, j):
        return pl.BlockSpec((ts, w), lambda b, s: (b * ns + s, j))

    halo = pl.BlockSpec((SUBLANES, 3 * width),
                        lambda b, s: (jnp.maximum((b * ns + s) * tiles_per_halo - 1, 0), 0))
    return pl.pallas_call(
        _gdn_kernel, grid=(batch, ns),
        in_specs=[row(3 * width, 0), halo, row(width, 3), row(LANES, 0), row(LANES, 1), row(d, 0),
                  _resident(conv_w.shape), _resident((1, LANES)), _resident((1, LANES)),
                  _resident((1, GDN_HEAD_DIM)), _resident(w_out.shape)],
        out_specs=row(d, 0), out_shape=jax.ShapeDtypeStruct((t, d), F32),
        scratch_shapes=[pltpu.VMEM((ts + SUBLANES, 3 * width), F32),
                        pltpu.VMEM((ts, width), F32), pltpu.VMEM((ts, width), F32),
                        pltpu.VMEM((ts, width), F32),
                        pltpu.VMEM((ts, LANES), F32), pltpu.VMEM((LANES, ts), F32),
                        pltpu.VMEM((ts, LANES), F32), pltpu.VMEM((ts, width), BF16),
                        pltpu.VMEM((GDN_HEADS, GDN_HEAD_DIM, GDN_HEAD_DIM), F32)],
        compiler_params=_params("parallel", "arbitrary"), name="gated_deltanet",
    )(proj, proj, proj, gates, gates, x, conv_w, lane_pad(a_log), lane_pad(dt_bias),
      o_norm.reshape(1, GDN_HEAD_DIM), w_out)


def _common_block(x, mem, xa_norm, mem_norm, xa_wq, xa_wkv, xa_wo, ffn_norm, ffn_w_up, ffn_conv,
                  ffn_w_down, final_gain, batch, seq, tm):
    (kv,) = norm_proj(mem, mem_norm, [xa_wkv.astype(BF16)], [BF16], tm=min(tm, mem.shape[0]))
    x = xattn(x, kv, xa_norm, xa_wq.astype(BF16), xa_wo.astype(BF16), batch, seq, tm)
    return conv_ffn(x, ffn_norm, ffn_w_up.astype(BF16), ffn_conv, ffn_w_down.astype(BF16), final_gain,
                    batch, seq, tm)


def kernel(x, mem, l0_mix_norm, l0_w_in, l0_ret_norm, l0_s5_lambda_re, l0_s5_lambda_im, l0_s5_b_re, l0_s5_b_im, l0_s5_c_re, l0_s5_c_im, l0_s5_d, l0_s5_log_dt, l0_s5_w_glu, l0_s5_b_glu, l0_w_out, l0_xa_norm, l0_mem_norm, l0_xa_wq, l0_xa_wkv, l0_xa_wo, l0_ffn_norm, l0_ffn_w_up, l0_ffn_conv, l0_ffn_w_down, l1_mix_norm, l1_w_in, l1_conv, l1_a_log, l1_dt_bias, l1_o_norm, l1_w_out, l1_xa_norm, l1_mem_norm, l1_xa_wq, l1_xa_wkv, l1_xa_wo, l1_ffn_norm, l1_ffn_w_up, l1_ffn_conv, l1_ffn_w_down, final_norm):
    batch, seq, d = x.shape
    tm = min(512, seq)
    xf = x.reshape(batch * seq, d)
    memf = mem.reshape(-1, d)
    ret_width = RET_HEADS * RET_HEAD_DIM

    (proj,) = norm_proj(xf, l0_mix_norm, [l0_w_in.astype(BF16)], [F32], tm)
    o_ret = retention(proj, l0_ret_norm, batch, seq, tm)
    y_s5 = s5_ssm(proj[:, 4 * ret_width:], l0_s5_lambda_re, l0_s5_lambda_im, l0_s5_b_re, l0_s5_b_im,
                  l0_s5_c_re, l0_s5_c_im, l0_s5_d, l0_s5_log_dt, batch, seq)
    xf = mix0_out(o_ret, y_s5, xf, l0_s5_w_glu.astype(BF16), l0_s5_b_glu, l0_w_out.astype(BF16), tm)
    xf = _common_block(xf, memf, l0_xa_norm, l0_mem_norm, l0_xa_wq, l0_xa_wkv, l0_xa_wo, l0_ffn_norm,
                       l0_ffn_w_up, l0_ffn_conv, l0_ffn_w_down, None, batch, seq, tm)

    gdn_width = GDN_HEADS * GDN_HEAD_DIM
    w_main = l1_w_in[:, :4 * gdn_width].astype(BF16)
    w_b = jnp.pad(l1_w_in[:, 4 * gdn_width:4 * gdn_width + GDN_HEADS], ((0, 0), (0, LANES - GDN_HEADS)))
    w_a = jnp.pad(l1_w_in[:, 4 * gdn_width + GDN_HEADS:], ((0, 0), (0, LANES - GDN_HEADS)))
    w_gates = jnp.concatenate([w_b, w_a], axis=1).astype(BF16)
    proj, gates = norm_proj(xf, l1_mix_norm, [w_main, w_gates], [F32, F32], tm)
    xf = gated_deltanet(proj, gates, xf, l1_conv, l1_a_log, l1_dt_bias, l1_o_norm, l1_w_out.astype(BF16),
                        batch, seq, tm)
    xf = _common_block(xf, memf, l1_xa_norm, l1_mem_norm, l1_xa_wq, l1_xa_wkv, l1_xa_wo, l1_ffn_norm,
                       l1_ffn_w_up, l1_ffn_conv, l1_ffn_w_down, final_norm, batch, seq, tm)
    return xf.reshape(batch, seq, d)
```

```python
import functools
import math

import jax
import jax.numpy as jnp
from jax import lax
from jax.experimental import pallas as pl
from jax.experimental.pallas import tpu as pltpu

F32 = jnp.float32
BF16 = jnp.bfloat16
EPS = 1e-6

V7X_VMEM_BYTES = 64 * 1024 * 1024
VMEM_LIMIT_BYTES = V7X_VMEM_BYTES - 8 * 1024 * 1024
LANES = 128
SUBLANES = 8

RET_HEADS = 4
RET_HEAD_DIM = 128
RET_CHUNK = 128
ROPE_BASE = 10000.0
S5_GROUP = 16
S5_STATE = 64
S5_CHUNK = 64
GDN_HEADS = 8
GDN_HEAD_DIM = 128
GDN_CONV = 4
GDN_CHUNK = 64
XA_HEADS = 4
FFN_CONV = 3
FFN_COLS = 256

HIGHEST = lax.Precision.HIGHEST


def _params(*semantics):
    return pltpu.CompilerParams(dimension_semantics=semantics, vmem_limit_bytes=VMEM_LIMIT_BYTES)


def _resident(shape):
    nd = len(shape)
    return pl.BlockSpec(shape, lambda *_: (0,) * nd)


def _dot(a, b, precision=None):
    return jnp.dot(a, b, preferred_element_type=F32, precision=precision)


def _dot_nt(a, b):
    return lax.dot_general(a, b, (((1,), (1,)), ((), ())), preferred_element_type=F32)


def _dot_tn(a, b):
    return lax.dot_general(a, b, (((0,), (0,)), ((), ())), preferred_element_type=F32)


def _rms(x, g):
    return x * lax.rsqrt(jnp.mean(x * x, axis=-1, keepdims=True) + EPS) * g


def _silu(x):
    return x * jax.nn.sigmoid(x)


def _norm_proj_kernel(x_ref, g_ref, *refs):
    n_out = len(refs) // 2
    xn = _rms(x_ref[...], g_ref[...]).astype(BF16)
    for w_ref, o_ref in zip(refs[:n_out], refs[n_out:]):
        o_ref[...] = _dot(xn, w_ref[...]).astype(o_ref.dtype)


def norm_proj(x, gain, weights, out_dtypes, tm):
    t, d = x.shape
    in_specs = [pl.BlockSpec((tm, d), lambda i: (i, 0)), _resident((1, d))]
    in_specs += [_resident(w.shape) for w in weights]
    out_specs = [pl.BlockSpec((tm, w.shape[1]), lambda i: (i, 0)) for w in weights]
    out_shape = [jax.ShapeDtypeStruct((t, w.shape[1]), dt) for w, dt in zip(weights, out_dtypes)]
    return pl.pallas_call(
        _norm_proj_kernel, grid=(t // tm,), in_specs=in_specs, out_specs=out_specs,
        out_shape=out_shape, compiler_params=_params("parallel"), name="norm_proj",
    )(x, gain.reshape(1, d), *weights)


def _retention_kernel(q_ref, k_ref, v_ref, gt_ref, cos_ref, sin_ref, intra_ref, qdec_ref, kdec_ref,
                      cdec_ref, rn_ref, o_ref, state_ref, *, n_chunks):
    c = RET_CHUNK

    @pl.when(pl.program_id(1) == 0)
    def _():
        state_ref[...] = jnp.zeros_like(state_ref)

    scale = RET_HEAD_DIM ** -0.5
    for ci in range(n_chunks):
        rows = pl.ds(ci * c, c)
        cos = cos_ref[rows, :]
        sin = sin_ref[rows, :]
        for h in range(RET_HEADS):
            cols = pl.ds(h * RET_HEAD_DIM, RET_HEAD_DIM)
            qh = q_ref[rows, cols]
            kh = k_ref[rows, cols]
            vh = v_ref[rows, cols].astype(BF16)
            qr = qh * cos + pltpu.roll(qh, RET_HEAD_DIM // 2, 1) * sin
            kr = (kh * cos + pltpu.roll(kh, RET_HEAD_DIM // 2, 1) * sin) * scale
            scores = _dot_nt(qr.astype(BF16), kr.astype(BF16)) * intra_ref[h]
            inner = _dot(scores.astype(BF16), vh)
            st = state_ref[h]
            cross = _dot((qr * qdec_ref[:, cols]).astype(BF16), st.astype(BF16))
            kv = _dot_tn((kr * kdec_ref[:, cols]).astype(BF16), vh)
            state_ref[h] = st * cdec_ref[:, cols] + kv
            o = inner + cross
            o = o * lax.rsqrt(jnp.mean(o * o, axis=-1, keepdims=True) + EPS)
            o_ref[rows, cols] = o * rn_ref[:, cols] * _silu(gt_ref[rows, cols])


def retention(proj, ret_norm, batch, seq, ts):
    t = proj.shape[0]
    width = RET_HEADS * RET_HEAD_DIM
    c = RET_CHUNK
    half = RET_HEAD_DIM // 2
    ns = seq // ts
    inv = jnp.exp(-math.log(ROPE_BASE) * jnp.arange(half, dtype=F32) / half)
    ang = jnp.arange(seq, dtype=F32)[:, None] * inv[None, :]
    cos = jnp.concatenate([jnp.cos(ang), jnp.cos(ang)], axis=-1)
    sin = jnp.concatenate([-jnp.sin(ang), jnp.sin(ang)], axis=-1)
    log_gamma = jnp.log1p(-jnp.exp2(-5.0 - jnp.arange(RET_HEADS, dtype=F32)))
    idx = jnp.arange(c, dtype=F32)
    diff = idx[:, None] - idx[None, :]
    causal = diff >= 0
    intra = jnp.where(causal, jnp.exp(log_gamma[:, None, None] * jnp.where(causal, diff, 0.0)), 0.0)

    def per_head_cols(tab):
        return jnp.repeat(tab.T, RET_HEAD_DIM, axis=1)

    qdec = per_head_cols(jnp.exp(log_gamma[:, None] * (idx + 1)))
    kdec = per_head_cols(jnp.exp(log_gamma[:, None] * (c - 1 - idx)))
    cdec = per_head_cols(jnp.exp(log_gamma * c)[:, None])

    def col_block(j):
        return pl.BlockSpec((ts, width), lambda b, s: (b * ns + s, j))

    in_specs = [col_block(0), col_block(1), col_block(2), col_block(3),
                pl.BlockSpec((ts, RET_HEAD_DIM), lambda b, s: (s, 0)),
                pl.BlockSpec((ts, RET_HEAD_DIM), lambda b, s: (s, 0)),
                _resident(intra.shape), _resident(qdec.shape), _resident(kdec.shape),
                _resident(cdec.shape), _resident((1, width))]
    return pl.pallas_call(
        functools.partial(_retention_kernel, n_chunks=ts // c),
        grid=(batch, ns), in_specs=in_specs,
        out_specs=pl.BlockSpec((ts, width), lambda b, s: (b * ns + s, 0)),
        out_shape=jax.ShapeDtypeStruct((t, width), F32),
        scratch_shapes=[pltpu.VMEM((RET_HEADS, RET_HEAD_DIM, RET_HEAD_DIM), F32)],
        compiler_params=_params("parallel", "arbitrary"), name="retention",
    )(proj, proj, proj, proj, cos, sin, intra, qdec, kdec, cdec, ret_norm.reshape(1, width))


def _s5_prep_kernel(lr_ref, li_ref, ldt_ref, lrc_ref, lic_ref, ldtc_ref, btr_ref, bti_ref, ctr_ref,
                    cti_ref, ktoe_ref, bdr_ref, bdi_ref, cdr_ref, cdi_ref, al_ref, *, chunk):
    hg = S5_GROUP
    n = chunk * hg
    lr, li = lr_ref[...], li_ref[...]
    dt = jnp.exp(ldt_ref[...])
    mag = jnp.exp(lr * dt)
    a_re = mag * jnp.cos(li * dt)
    a_im = mag * jnp.sin(li * dt)
    den = lr * lr + li * li
    z_re = ((a_re - 1.0) * lr + a_im * li) / den
    z_im = (a_im * lr - (a_re - 1.0) * li) / den
    btr, bti = btr_ref[...], bti_ref[...]
    bb_re = z_re * btr - z_im * bti
    bb_im = z_re * bti + z_im * btr
    j = (lax.broadcasted_iota(jnp.int32, (n, S5_STATE), 0) // hg).astype(F32)
    e = (chunk - 1.0) - j
    pm = jnp.exp(lr * dt * e)
    pr = pm * jnp.cos(li * dt * e)
    pi = pm * jnp.sin(li * dt * e)
    bdr_ref[...] = (bb_re * pr - bb_im * pi).astype(bdr_ref.dtype)
    bdi_ref[...] = (bb_re * pi + bb_im * pr).astype(bdi_ref.dtype)
    lm = jnp.exp(lr * dt * chunk)
    al_ref[0:1, :] = lm * jnp.cos(li * dt * chunk)
    al_ref[1:2, :] = lm * jnp.sin(li * dt * chunk)

    lrc, lic = lrc_ref[...], lic_ref[...]
    dtc = jnp.exp(ldtc_ref[...])
    tau = (lax.broadcasted_iota(jnp.int32, (S5_STATE, n), 1) // hg).astype(F32)
    vm = jnp.exp(lrc * dtc * tau)
    vc = vm * jnp.cos(lic * dtc * tau)
    vs = vm * jnp.sin(lic * dtc * tau)
    ctr, cti = ctr_ref[...], cti_ref[...]
    v_re = vc * ctr - vs * cti
    v_im = vc * cti + vs * ctr
    magc = jnp.exp(lrc * dtc)
    ac_re = magc * jnp.cos(lic * dtc)
    ac_im = magc * jnp.sin(lic * dtc)
    cdr_ref[...] = (v_re * ac_re - v_im * ac_im).astype(cdr_ref.dtype)
    cdi_ref[...] = (-(v_re * ac_im + v_im * ac_re)).astype(cdi_ref.dtype)

    p = (_dot(bb_re[0:hg, :], v_re, HIGHEST) - _dot(bb_im[0:hg, :], v_im, HIGHEST))
    lane = lax.broadcasted_iota(jnp.int32, (hg, n), 1)
    for jj in range(chunk):
        if jj == 0:
            blk = p
        else:
            blk = jnp.where(lane >= jj * hg, pltpu.roll(p, jj * hg, 1), 0.0)
        ktoe_ref[jj * hg:(jj + 1) * hg, :] = blk.astype(ktoe_ref.dtype)


def _s5_main_kernel(u_ref, ktoe_ref, bdr_ref, bdi_ref, cdr_ref, cdi_ref, al_ref, d_ref, y_ref,
                    sre_ref, sim_ref, *, batch, n_chunks):
    u = u_ref[...]
    ub = u.astype(BF16)
    e_re = _dot(ub, bdr_ref[...])
    e_im = _dot(ub, bdi_ref[...])
    a_re = al_ref[0:1, :]
    a_im = al_ref[1:2, :]
    sre_ref[0:batch, :] = jnp.zeros((batch, S5_STATE), F32)
    sim_ref[0:batch, :] = jnp.zeros((batch, S5_STATE), F32)
    sre_ref[batch:, :] = e_re[:-batch, :]
    sim_ref[batch:, :] = e_im[:-batch, :]

    def step(nn, carry):
        prev = pl.ds(pl.multiple_of((nn - 1) * batch, batch), batch)
        cur = pl.ds(pl.multiple_of(nn * batch, batch), batch)
        pr, pi = sre_ref[prev, :], sim_ref[prev, :]
        sre_ref[cur, :] = sre_ref[cur, :] + a_re * pr - a_im * pi
        sim_ref[cur, :] = sim_ref[cur, :] + a_re * pi + a_im * pr
        return carry

    lax.fori_loop(1, n_chunks, step, 0)
    y = _dot(ub, ktoe_ref[...])
    y += _dot(sre_ref[...].astype(BF16), cdr_ref[...])
    y += _dot(sim_ref[...].astype(BF16), cdi_ref[...])
    y_ref[...] = y + d_ref[...] * u


def s5_ssm(u, lam_re, lam_im, b_re, b_im, c_re, c_im, d, log_dt, batch, seq):
    t, width = u.shape
    g, hg, p, chunk = width // S5_GROUP, S5_GROUP, S5_STATE, S5_CHUNK
    n = chunk * hg
    nchunks = seq // chunk
    rows = nchunks * batch

    def grp(shape):
        return pl.BlockSpec((None,) + shape, lambda i: (i, 0, 0))

    ldt = jnp.broadcast_to(log_dt[:, None, None], (g, 1, p))
    prep_in = [lam_re[:, None, :], lam_im[:, None, :], ldt,
               lam_re[:, :, None], lam_im[:, :, None], jnp.swapaxes(ldt, 1, 2),
               jnp.tile(b_re, (1, chunk, 1)), jnp.tile(b_im, (1, chunk, 1)),
               jnp.tile(c_re, (1, 1, chunk)), jnp.tile(c_im, (1, 1, chunk))]
    prep_specs = [grp((1, p))] * 3 + [grp((p, 1))] * 3 + [grp((n, p))] * 2 + [grp((p, n))] * 2
    ktoe, bdr, bdi, cdr, cdi, al = pl.pallas_call(
        functools.partial(_s5_prep_kernel, chunk=chunk),
        grid=(g,), in_specs=prep_specs,
        out_specs=[grp((n, n)), grp((n, p)), grp((n, p)), grp((p, n)), grp((p, n)), grp((2, p))],
        out_shape=[jax.ShapeDtypeStruct((g, n, n), BF16),
                   jax.ShapeDtypeStruct((g, n, p), BF16), jax.ShapeDtypeStruct((g, n, p), BF16),
                   jax.ShapeDtypeStruct((g, p, n), BF16), jax.ShapeDtypeStruct((g, p, n), BF16),
                   jax.ShapeDtypeStruct((g, 2, p), F32)],
        compiler_params=_params("parallel"), name="s5_prep",
    )(*prep_in)

    ug = u.reshape(batch, nchunks, chunk, g, hg).transpose(3, 1, 0, 2, 4).reshape(g, rows, n)
    dvec = jnp.tile(d[:, None, :], (1, 1, chunk))
    yg = pl.pallas_call(
        functools.partial(_s5_main_kernel, batch=batch, n_chunks=nchunks),
        grid=(g,),
        in_specs=[grp((rows, n)), grp((n, n)), grp((n, p)), grp((n, p)), grp((p, n)), grp((p, n)),
                  grp((2, p)), grp((1, n))],
        out_specs=grp((rows, n)),
        out_shape=jax.ShapeDtypeStruct((g, rows, n), F32),
        scratch_shapes=[pltpu.VMEM((rows, p), F32), pltpu.VMEM((rows, p), F32)],
        compiler_params=_params("parallel"), name="s5_main",
    )(ug, ktoe, bdr, bdi, cdr, cdi, al, dvec)
    return yg.reshape(g, nchunks, batch, chunk, hg).transpose(2, 1, 3, 0, 4).reshape(t, width)


def _mix0_out_kernel(o_ref, y_ref, x_ref, wglu_ref, bglu_ref, wout_ref, out_ref):
    y = y_ref[...]
    y = 0.5 * y * (1.0 + jnp.tanh(math.sqrt(2.0 / math.pi) * (y + 0.044715 * (y * y * y))))
    y = y * jax.nn.sigmoid(_dot(y.astype(BF16), wglu_ref[...]) + bglu_ref[...])
    merged = jnp.concatenate([o_ref[...].astype(BF16), y.astype(BF16)], axis=-1)
    out_ref[...] = x_ref[...] + _dot(merged, wout_ref[...])


def mix0_out(o, y, x, w_glu, b_glu, w_out, tm):
    t, d = x.shape
    half = o.shape[1]
    row = lambda w: pl.BlockSpec((tm, w), lambda i: (i, 0))
    return pl.pallas_call(
        _mix0_out_kernel, grid=(t // tm,),
        in_specs=[row(half), row(half), row(d), _resident(w_glu.shape), _resident((1, half)),
                  _resident(w_out.shape)],
        out_specs=row(d), out_shape=jax.ShapeDtypeStruct((t, d), F32),
        compiler_params=_params("parallel"), name="mix0_out",
    )(o, y, x, w_glu, b_glu.reshape(1, half), w_out)


def _xattn_kernel(x_ref, g_ref, wq_ref, kv_ref, wo_ref, o_ref):
    x = x_ref[...]
    d = x.shape[1]
    dh = d // XA_HEADS
    xn = _rms(x, g_ref[...]).astype(BF16)
    q = _dot(xn, wq_ref[...]).astype(BF16)
    heads = []
    for h in range(XA_HEADS):
        kh = kv_ref[:, pl.ds(h * dh, dh)]
        vh = kv_ref[:, pl.ds(d + h * dh, dh)]
        s = _dot_nt(q[:, h * dh:(h + 1) * dh], kh) * (dh ** -0.5)
        e = jnp.exp(s - jnp.max(s, axis=-1, keepdims=True))
        p = e * (1.0 / jnp.sum(e, axis=-1, keepdims=True))
        heads.append(_dot(p.astype(BF16), vh).astype(BF16))
    o_ref[...] = x + _dot(jnp.concatenate(heads, axis=-1), wo_ref[...])


def xattn(x, kv, gain, wq, wo, batch, seq, tm):
    t, d = x.shape
    n_mem = kv.shape[0] // batch
    ns = seq // tm
    row = pl.BlockSpec((tm, d), lambda b, s: (b * ns + s, 0))
    return pl.pallas_call(
        _xattn_kernel, grid=(batch, ns),
        in_specs=[row, _resident((1, d)), _resident(wq.shape),
                  pl.BlockSpec((n_mem, 2 * d), lambda b, s: (b, 0)), _resident(wo.shape)],
        out_specs=row, out_shape=jax.ShapeDtypeStruct((t, d), F32),
        compiler_params=_params("parallel", "parallel"), name="xattn",
    )(x, gain.reshape(1, d), wq, kv, wo)


def _ffn_kernel(x_ref, g_ref, wup_ref, cw_ref, wd_ref, fg_ref, o_ref, xn_ref, tail_ref, act_ref,
                *, final_norm):
    tm = x_ref.shape[0]
    f = wd_ref.shape[0]
    fc = FFN_COLS
    pad = SUBLANES

    @pl.when(pl.program_id(1) == 0)
    def _():
        tail_ref[...] = jnp.zeros_like(tail_ref)

    xn_ref[...] = _rms(x_ref[...], g_ref[...]).astype(BF16)

    def proj_conv(col0):
        cols = pl.ds(col0, fc)
        cur = _dot(xn_ref[...], wup_ref[:, cols])
        z = jnp.concatenate([tail_ref[:, cols], cur], axis=0)
        tail_ref[:, cols] = cur[tm - pad:, :]
        y = cw_ref[0:1, cols] * z
        y = cw_ref[1:2, cols] * z + pltpu.roll(y, 1, 0)
        y = cw_ref[2:3, cols] * z + pltpu.roll(y, 1, 0)
        return y[pad:, :]

    for cb in range(f // fc):
        up = proj_conv(cb * fc)
        gate = proj_conv(f + cb * fc)
        act_ref[:, pl.ds(cb * fc, fc)] = (_silu(gate) * up).astype(BF16)
    y = x_ref[...] + _dot(act_ref[...], wd_ref[...])
    if final_norm:
        y = _rms(y, fg_ref[...])
    o_ref[...] = y


def conv_ffn(x, gain, w_up, conv_w, w_down, final_gain, batch, seq, tm):
    t, d = x.shape
    f = w_down.shape[0]
    assert FFN_CONV == conv_w.shape[0] == 3 and f % FFN_COLS == 0
    ns = seq // tm
    final_norm = final_gain is not None
    fg = (final_gain if final_norm else jnp.ones((d,), F32)).reshape(1, d)
    row = pl.BlockSpec((tm, d), lambda b, s: (b * ns + s, 0))
    return pl.pallas_call(
        functools.partial(_ffn_kernel, final_norm=final_norm), grid=(batch, ns),
        in_specs=[row, _resident((1, d)), _resident(w_up.shape), _resident(conv_w.shape),
                  _resident(w_down.shape), _resident((1, d))],
        out_specs=row, out_shape=jax.ShapeDtypeStruct((t, d), F32),
        scratch_shapes=[pltpu.VMEM((tm, d), BF16), pltpu.VMEM((SUBLANES, 2 * f), F32),
                        pltpu.VMEM((tm, f), BF16)],
        compiler_params=_params("parallel", "arbitrary"), name="conv_ffn",
    )(x, gain.reshape(1, d), w_up, conv_w, w_down, fg)


def _unit_lower_inverses(a_list, eye):
    n = eye.shape[0]
    ms = [eye - a for a in a_list]
    ps = [(-a).astype(BF16) for a in a_list]
    ps = [_dot(p, p).astype(BF16) for p in ps]
    for _ in range(4):
        both = [_dot(jnp.concatenate([m.astype(BF16), p], axis=0), p) for m, p in zip(ms, ps)]
        ms = [m + b[0:n] for m, b in zip(ms, both)]
        ps = [b[n:].astype(BF16) for b in both]
    return [m + _dot(m.astype(BF16), p) for m, p in zip(ms, ps)]


def _gdn_proj_kernel(x_ref, g_ref, wqkv_ref, wz_ref, wg_ref, cw_ref, qkv_ref, z_ref, gates_ref,
                     xn_ref, tail_ref):
    tm = x_ref.shape[0]
    width = GDN_HEADS * GDN_HEAD_DIM
    dh = GDN_HEAD_DIM
    cb_cols = 2 * dh
    pad = SUBLANES

    @pl.when(pl.program_id(1) == 0)
    def _():
        tail_ref[...] = jnp.zeros_like(tail_ref)

    xn_ref[...] = _rms(x_ref[...], g_ref[...]).astype(BF16)
    for cb in range(3 * width // cb_cols):
        cols = pl.ds(cb * cb_cols, cb_cols)
        cur = _dot(xn_ref[...], wqkv_ref[:, cols])
        zz = jnp.concatenate([tail_ref[:, cols], cur], axis=0)
        tail_ref[:, cols] = cur[tm - pad:, :]
        zs = pltpu.roll(zz, 1, 0)
        near = cw_ref[3:4, cols] * zz + cw_ref[2:3, cols] * zs
        far = cw_ref[1:2, cols] * zz + cw_ref[0:1, cols] * zs
        y = _silu((near + pltpu.roll(far, 2, 0))[pad:, :])
        if cb * cb_cols < 2 * width:
            scale = dh ** -0.5 if cb * cb_cols < width else 1.0
            parts = []
            for j in range(cb_cols // dh):
                yh = y[:, j * dh:(j + 1) * dh]
                parts.append(yh * (lax.rsqrt(jnp.sum(yh * yh, axis=-1, keepdims=True) + EPS) * scale))
            y = jnp.concatenate(parts, axis=1)
        qkv_ref[:, cols] = y
    z_ref[...] = _dot(xn_ref[...], wz_ref[...])
    gates_ref[...] = _dot(xn_ref[...], wg_ref[...])


def gdn_proj(x, gain, w_qkv, w_z, w_gates, conv_w, batch, seq, tm):
    t, d = x.shape
    ns = seq // tm
    assert conv_w.shape[0] == GDN_CONV

    def row(w):
        return pl.BlockSpec((tm, w), lambda b, s: (b * ns + s, 0))

    widths = [w_qkv.shape[1], w_z.shape[1], w_gates.shape[1]]
    return pl.pallas_call(
        _gdn_proj_kernel, grid=(batch, ns),
        in_specs=[row(d), _resident((1, d)), _resident(w_qkv.shape), _resident(w_z.shape),
                  _resident(w_gates.shape), _resident(conv_w.shape)],
        out_specs=[row(w) for w in widths],
        out_shape=[jax.ShapeDtypeStruct((t, w), F32) for w in widths],
        scratch_shapes=[pltpu.VMEM((tm, d), BF16), pltpu.VMEM((SUBLANES, w_qkv.shape[1]), F32)],
        compiler_params=_params("parallel", "arbitrary"), name="gdn_proj",
    )(x, gain.reshape(1, d), w_qkv, w_z, w_gates, conv_w)


def _gdn_kernel(qkv_ref, z_ref, b_ref, a_ref, x_ref, alog_ref, dtb_ref, on_ref, wout_ref, o_ref,
                q_ref, k_ref, v_ref, w_ref, qk_ref, gc_ref, gct_ref, beta_ref, ob_ref, state_ref):
    ts = qkv_ref.shape[0]
    width = GDN_HEADS * GDN_HEAD_DIM
    dh = GDN_HEAD_DIM
    c = GDN_CHUNK
    heads = range(GDN_HEADS)

    @pl.when(pl.program_id(1) == 0)
    def _():
        state_ref[...] = jnp.zeros_like(state_ref)

    beta_ref[...] = jax.nn.sigmoid(b_ref[...])
    ag = a_ref[...] + dtb_ref[...]
    softplus = jnp.maximum(ag, 0.0) + jnp.log1p(jnp.exp(-jnp.abs(ag)))
    gc = -(jnp.exp(alog_ref[...]) * softplus)
    pos = lax.broadcasted_iota(jnp.int32, (ts, LANES), 0) % c
    shift = 1
    while shift < c:
        gc = gc + jnp.where(pos >= shift, pltpu.roll(gc, shift, 0), 0.0)
        shift *= 2
    gc_ref[...] = gc
    gct_ref[...] = gc.T

    ri = lax.broadcasted_iota(jnp.int32, (2 * c, 2 * c), 0)
    ci = lax.broadcasted_iota(jnp.int32, (2 * c, 2 * c), 1)
    same = (ri // c) == (ci // c)
    incl = same & (ri >= ci)
    strict = same & (ri > ci)
    eye = jnp.where(ri == ci, 1.0, 0.0).astype(F32)

    def hcols(h):
        return pl.ds(h * dh, dh)

    def wy_factors(dc, carry):
        r0 = pl.multiple_of(dc * 2 * c, 2 * c)
        rows = pl.ds(r0, 2 * c)
        gcd = gc_ref[rows, :]
        gctd = gct_ref[:, rows]
        bet = beta_ref[rows, :]
        qs = [qkv_ref[rows, hcols(h)] for h in heads]
        ks = [qkv_ref[rows, hcols(GDN_HEADS + h)] for h in heads]
        vs = [qkv_ref[rows, hcols(2 * GDN_HEADS + h)] for h in heads]
        gcols = [gcd[:, h:h + 1] for h in heads]
        bcols = [bet[:, h:h + 1] for h in heads]
        decays = [jnp.where(incl, jnp.exp(jnp.where(incl, gcols[h] - gctd[h:h + 1, :], 0.0)), 0.0)
                  for h in heads]
        kbs = [ks[h] * bcols[h] for h in heads]
        kbf = [k.astype(BF16) for k in ks]
        a_mats = [jnp.where(strict, _dot_nt(kbs[h].astype(BF16), kbf[h]) * decays[h], 0.0) for h in heads]
        for h in heads:
            qk = jnp.where(incl, _dot_nt(qs[h].astype(BF16), kbf[h]) * decays[h], 0.0)
            qk_ref[rows, hcols(h)] = qk.astype(BF16)
        t_mats = [t.astype(BF16) for t in _unit_lower_inverses(a_mats, eye)]
        egs = [jnp.exp(g) for g in gcols]
        for h in heads:
            w_ref[rows, hcols(h)] = _dot(t_mats[h], (kbs[h] * egs[h]).astype(BF16)).astype(BF16)
        for h in heads:
            v_ref[rows, hcols(h)] = _dot(t_mats[h], (vs[h] * bcols[h]).astype(BF16))
        for h in heads:
            q_ref[rows, hcols(h)] = (qs[h] * egs[h]).astype(BF16)
            gl0 = gcd[c - 1:c, h:h + 1]
            gl1 = gcd[2 * c - 1:2 * c, h:h + 1]
            glcol = jnp.concatenate([jnp.broadcast_to(gl0, (c, 1)), jnp.broadcast_to(gl1, (c, 1))], axis=0)
            k_ref[rows, hcols(h)] = (ks[h] * jnp.exp(glcol - gcols[h])).astype(BF16)
        return carry

    lax.fori_loop(0, ts // (2 * c), wy_factors, 0)

    def recurrence(dc, carry):
        r0 = pl.multiple_of(dc * 2 * c, 2 * c)
        rows = pl.ds(r0, 2 * c)
        sts = [state_ref[h] for h in heads]
        vns, outs = [], []
        for half in range(2):
            r = pl.ds(pl.multiple_of(r0 + half * c, c), c)
            egl = jnp.exp(gc_ref[pl.ds(r0 + half * c + c - 1, 1), :])
            sbs = [s.astype(BF16) for s in sts]
            vn = [v_ref[r, hcols(h)] - _dot(w_ref[r, hcols(h)], sbs[h]) for h in heads]
            outs.append([_dot(q_ref[r, hcols(h)], sbs[h]) for h in heads])
            sts = [sts[h] * egl[:, h:h + 1] + _dot_tn(k_ref[r, hcols(h)], vn[h].astype(BF16))
                   for h in heads]
            vns.append(vn)
        for h in heads:
            state_ref[h] = sts[h]
        for h in heads:
            vn = jnp.concatenate([vns[0][h], vns[1][h]], axis=0).astype(BF16)
            o = jnp.concatenate([outs[0][h], outs[1][h]], axis=0) + _dot(qk_ref[rows, hcols(h)], vn)
            o = o * lax.rsqrt(jnp.mean(o * o, axis=-1, keepdims=True) + EPS) * on_ref[...]
            ob_ref[rows, hcols(h)] = (o * _silu(z_ref[rows, hcols(h)])).astype(BF16)
        return carry

    lax.fori_loop(0, ts // (2 * c), recurrence, 0)
    o_ref[...] = x_ref[...] + _dot(ob_ref[...], wout_ref[...])


def gated_deltanet(qkv, z, gates, x, a_log, dt_bias, o_norm, w_out, batch, seq, ts):
    t, d = x.shape
    width = GDN_HEADS * GDN_HEAD_DIM
    ns = seq // ts

    def lane_pad(vec):
        return jnp.pad(vec, (0, LANES - vec.shape[0])).reshape(1, LANES)

    def row(w, j):
        return pl.BlockSpec((ts, w), lambda b, s: (b * ns + s, j))

    return pl.pallas_call(
        _gdn_kernel, grid=(batch, ns),
        in_specs=[row(3 * width, 0), row(width, 0), row(LANES, 0), row(LANES, 1), row(d, 0),
                  _resident((1, LANES)), _resident((1, LANES)), _resident((1, GDN_HEAD_DIM)),
                  _resident(w_out.shape)],
        out_specs=row(d, 0), out_shape=jax.ShapeDtypeStruct((t, d), F32),
        scratch_shapes=[pltpu.VMEM((ts, width), BF16), pltpu.VMEM((ts, width), BF16),
                        pltpu.VMEM((ts, width), F32),
                        pltpu.VMEM((ts, width), BF16), pltpu.VMEM((ts, width), BF16),
                        pltpu.VMEM((ts, LANES), F32), pltpu.VMEM((LANES, ts), F32),
                        pltpu.VMEM((ts, LANES), F32), pltpu.VMEM((ts, width), BF16),
                        pltpu.VMEM((GDN_HEADS, GDN_HEAD_DIM, GDN_HEAD_DIM), F32)],
        compiler_params=_params("parallel", "arbitrary"), name="gated_deltanet",
    )(qkv, z, gates, gates, x, lane_pad(a_log), lane_pad(dt_bias), o_norm.reshape(1, GDN_HEAD_DIM), w_out)


def _common_block(x, mem, xa_norm, mem_norm, xa_wq, xa_wkv, xa_wo, ffn_norm, ffn_w_up, ffn_conv,
                  ffn_w_down, final_gain, batch, seq, tm):
    (kv,) = norm_proj(mem, mem_norm, [xa_wkv.astype(BF16)], [BF16], tm=min(tm, mem.shape[0]))
    x = xattn(x, kv, xa_norm, xa_wq.astype(BF16), xa_wo.astype(BF16), batch, seq, tm)
    return conv_ffn(x, ffn_norm, ffn_w_up.astype(BF16), ffn_conv, ffn_w_down.astype(BF16), final_gain,
                    batch, seq, tm)


def kernel(x, mem, l0_mix_norm, l0_w_in, l0_ret_norm, l0_s5_lambda_re, l0_s5_lambda_im, l0_s5_b_re, l0_s5_b_im, l0_s5_c_re, l0_s5_c_im, l0_s5_d, l0_s5_log_dt, l0_s5_w_glu, l0_s5_b_glu, l0_w_out, l0_xa_norm, l0_mem_norm, l0_xa_wq, l0_xa_wkv, l0_xa_wo, l0_ffn_norm, l0_ffn_w_up, l0_ffn_conv, l0_ffn_w_down, l1_mix_norm, l1_w_in, l1_conv, l1_a_log, l1_dt_bias, l1_o_norm, l1_w_out, l1_xa_norm, l1_mem_norm, l1_xa_wq, l1_xa_wkv, l1_xa_wo, l1_ffn_norm, l1_ffn_w_up, l1_ffn_conv, l1_ffn_w_down, final_norm):
    batch, seq, d = x.shape
    tm = min(512, seq)
    xf = x.reshape(batch * seq, d)
    memf = mem.reshape(-1, d)
    ret_width = RET_HEADS * RET_HEAD_DIM

    (proj,) = norm_proj(xf, l0_mix_norm, [l0_w_in.astype(BF16)], [F32], tm)
    o_ret = retention(proj, l0_ret_norm, batch, seq, tm)
    y_s5 = s5_ssm(proj[:, 4 * ret_width:], l0_s5_lambda_re, l0_s5_lambda_im, l0_s5_b_re, l0_s5_b_im,
                  l0_s5_c_re, l0_s5_c_im, l0_s5_d, l0_s5_log_dt, batch, seq)
    xf = mix0_out(o_ret, y_s5, xf, l0_s5_w_glu.astype(BF16), l0_s5_b_glu, l0_w_out.astype(BF16), tm)
    xf = _common_block(xf, memf, l0_xa_norm, l0_mem_norm, l0_xa_wq, l0_xa_wkv, l0_xa_wo, l0_ffn_norm,
                       l0_ffn_w_up, l0_ffn_conv, l0_ffn_w_down, None, batch, seq, tm)

    gdn_width = GDN_HEADS * GDN_HEAD_DIM
    w_qkv = l1_w_in[:, :3 * gdn_width].astype(BF16)
    w_z = l1_w_in[:, 3 * gdn_width:4 * gdn_width].astype(BF16)
    w_b = jnp.pad(l1_w_in[:, 4 * gdn_width:4 * gdn_width + GDN_HEADS], ((0, 0), (0, LANES - GDN_HEADS)))
    w_a = jnp.pad(l1_w_in[:, 4 * gdn_width + GDN_HEADS:], ((0, 0), (0, LANES - GDN_HEADS)))
    w_gates = jnp.concatenate([w_b, w_a], axis=1).astype(BF16)
    qkv, z, gates = gdn_proj(xf, l1_mix_norm, w_qkv, w_z, w_gates, l1_conv, batch, seq, tm)
    xf = gated_deltanet(qkv, z, gates, xf, l1_a_log, l1_dt_bias, l1_o_norm, l1_w_out.astype(BF16),
                        batch, seq, tm)
    xf = _common_block(xf, memf, l1_xa_norm, l1_mem_norm, l1_xa_wq, l1_xa_wkv, l1_xa_wo, l1_ffn_norm,
                       l1_ffn_w_up, l1_ffn_conv, l1_ffn_w_down, final_norm, batch, seq, tm)
    return xf.reshape(batch, seq, d)
```

```python
import functools
import math

import jax
import jax.numpy as jnp
from jax import lax
from jax.experimental import pallas as pl
from jax.experimental.pallas import tpu as pltpu

F32 = jnp.float32
BF16 = jnp.bfloat16
EPS = 1e-6

V7X_VMEM_BYTES = 64 * 1024 * 1024
VMEM_LIMIT_BYTES = V7X_VMEM_BYTES - 8 * 1024 * 1024
LANES = 128
SUBLANES = 8

RET_HEADS = 4
RET_HEAD_DIM = 128
RET_CHUNK = 128
ROPE_BASE = 10000.0
S5_GROUP = 16
S5_STATE = 64
S5_CHUNK = 64
GDN_HEADS = 8
GDN_HEAD_DIM = 128
GDN_CONV = 4
GDN_CHUNK = 64
XA_HEADS = 4
FFN_CONV = 3
FFN_COLS = 256

HIGHEST = lax.Precision.HIGHEST


def _params(*semantics):
    return pltpu.CompilerParams(dimension_semantics=semantics, vmem_limit_bytes=VMEM_LIMIT_BYTES)


def _resident(shape):
    nd = len(shape)
    return pl.BlockSpec(shape, lambda *_: (0,) * nd)


def _dot(a, b, precision=None):
    return jnp.dot(a, b, preferred_element_type=F32, precision=precision)


def _dot_nt(a, b):
    return lax.dot_general(a, b, (((1,), (1,)), ((), ())), preferred_element_type=F32)


def _dot_tn(a, b):
    return lax.dot_general(a, b, (((0,), (0,)), ((), ())), preferred_element_type=F32)


def _rms(x, g):
    return x * lax.rsqrt(jnp.mean(x * x, axis=-1, keepdims=True) + EPS) * g


def _silu(x):
    return x * jax.nn.sigmoid(x)


def _norm_proj_kernel(x_ref, g_ref, *refs):
    n_out = len(refs) // 2
    xn = _rms(x_ref[...], g_ref[...]).astype(BF16)
    for w_ref, o_ref in zip(refs[:n_out], refs[n_out:]):
        o_ref[...] = _dot(xn, w_ref[...]).astype(o_ref.dtype)


def norm_proj(x, gain, weights, out_dtypes, tm):
    t, d = x.shape
    in_specs = [pl.BlockSpec((tm, d), lambda i: (i, 0)), _resident((1, d))]
    in_specs += [_resident(w.shape) for w in weights]
    out_specs = [pl.BlockSpec((tm, w.shape[1]), lambda i: (i, 0)) for w in weights]
    out_shape = [jax.ShapeDtypeStruct((t, w.shape[1]), dt) for w, dt in zip(weights, out_dtypes)]
    return pl.pallas_call(
        _norm_proj_kernel, grid=(t // tm,), in_specs=in_specs, out_specs=out_specs,
        out_shape=out_shape, compiler_params=_params("parallel"), name="norm_proj",
    )(x, gain.reshape(1, d), *weights)


def _even_proj_kernel(x_ref, g_ref, w_ref, wut_ref, o_ref, u3_ref):
    xn = _rms(x_ref[...], g_ref[...]).astype(BF16)
    o_ref[...] = _dot(xn, w_ref[...])
    ut = _dot_nt(wut_ref[...], xn)
    for k in range(u3_ref.shape[0]):
        u3_ref[k] = ut[:, k * LANES:(k + 1) * LANES]


def even_proj(x, gain, w, w_u_t, tm):
    t, d = x.shape
    n, nu = w.shape[1], w_u_t.shape[0]
    return pl.pallas_call(
        _even_proj_kernel, grid=(t // tm,),
        in_specs=[pl.BlockSpec((tm, d), lambda i: (i, 0)), _resident((1, d)), _resident(w.shape),
                  _resident(w_u_t.shape)],
        out_specs=[pl.BlockSpec((tm, n), lambda i: (i, 0)),
                   pl.BlockSpec((tm // LANES, nu, LANES), lambda i: (i, 0, 0))],
        out_shape=[jax.ShapeDtypeStruct((t, n), F32), jax.ShapeDtypeStruct((t // LANES, nu, LANES), F32)],
        compiler_params=_params("parallel"), name="even_proj",
    )(x, gain.reshape(1, d), w, w_u_t)


def _retention_kernel(q_ref, k_ref, v_ref, gt_ref, cos_ref, sin_ref, intra_ref, qdec_ref, kdec_ref,
                      cdec_ref, rn_ref, o_ref, state_ref, *, n_chunks):
    c = RET_CHUNK

    @pl.when(pl.program_id(1) == 0)
    def _():
        state_ref[...] = jnp.zeros_like(state_ref)

    scale = RET_HEAD_DIM ** -0.5
    for ci in range(n_chunks):
        rows = pl.ds(ci * c, c)
        cos = cos_ref[rows, :]
        sin = sin_ref[rows, :]
        for h in range(RET_HEADS):
            cols = pl.ds(h * RET_HEAD_DIM, RET_HEAD_DIM)
            qh = q_ref[rows, cols]
            kh = k_ref[rows, cols]
            vh = v_ref[rows, cols].astype(BF16)
            qr = qh * cos + pltpu.roll(qh, RET_HEAD_DIM // 2, 1) * sin
            kr = (kh * cos + pltpu.roll(kh, RET_HEAD_DIM // 2, 1) * sin) * scale
            scores = _dot_nt(qr.astype(BF16), kr.astype(BF16)) * intra_ref[h]
            inner = _dot(scores.astype(BF16), vh)
            st = state_ref[h]
            cross = _dot((qr * qdec_ref[:, cols]).astype(BF16), st.astype(BF16))
            kv = _dot_tn((kr * kdec_ref[:, cols]).astype(BF16), vh)
            state_ref[h] = st * cdec_ref[:, cols] + kv
            o = inner + cross
            o = o * lax.rsqrt(jnp.mean(o * o, axis=-1, keepdims=True) + EPS)
            o_ref[rows, cols] = o * rn_ref[:, cols] * _silu(gt_ref[rows, cols])


def retention(proj, ret_norm, batch, seq, ts):
    t = proj.shape[0]
    width = RET_HEADS * RET_HEAD_DIM
    c = RET_CHUNK
    half = RET_HEAD_DIM // 2
    ns = seq // ts
    inv = jnp.exp(-math.log(ROPE_BASE) * jnp.arange(half, dtype=F32) / half)
    ang = jnp.arange(seq, dtype=F32)[:, None] * inv[None, :]
    cos = jnp.concatenate([jnp.cos(ang), jnp.cos(ang)], axis=-1)
    sin = jnp.concatenate([-jnp.sin(ang), jnp.sin(ang)], axis=-1)
    log_gamma = jnp.log1p(-jnp.exp2(-5.0 - jnp.arange(RET_HEADS, dtype=F32)))
    idx = jnp.arange(c, dtype=F32)
    diff = idx[:, None] - idx[None, :]
    causal = diff >= 0
    intra = jnp.where(causal, jnp.exp(log_gamma[:, None, None] * jnp.where(causal, diff, 0.0)), 0.0)

    def per_head_cols(tab):
        return jnp.repeat(tab.T, RET_HEAD_DIM, axis=1)

    qdec = per_head_cols(jnp.exp(log_gamma[:, None] * (idx + 1)))
    kdec = per_head_cols(jnp.exp(log_gamma[:, None] * (c - 1 - idx)))
    cdec = per_head_cols(jnp.exp(log_gamma * c)[:, None])

    def col_block(j):
        return pl.BlockSpec((ts, width), lambda b, s: (b * ns + s, j))

    in_specs = [col_block(0), col_block(1), col_block(2), col_block(3),
                pl.BlockSpec((ts, RET_HEAD_DIM), lambda b, s: (s, 0)),
                pl.BlockSpec((ts, RET_HEAD_DIM), lambda b, s: (s, 0)),
                _resident(intra.shape), _resident(qdec.shape), _resident(kdec.shape),
                _resident(cdec.shape), _resident((1, width))]
    return pl.pallas_call(
        functools.partial(_retention_kernel, n_chunks=ts // c),
        grid=(batch, ns), in_specs=in_specs,
        out_specs=pl.BlockSpec((ts, width), lambda b, s: (b * ns + s, 0)),
        out_shape=jax.ShapeDtypeStruct((t, width), F32),
        scratch_shapes=[pltpu.VMEM((RET_HEADS, RET_HEAD_DIM, RET_HEAD_DIM), F32)],
        compiler_params=_params("parallel", "arbitrary"), name="retention",
    )(proj, proj, proj, proj, cos, sin, intra, qdec, kdec, cdec, ret_norm.reshape(1, width))


def _s5_prep_kernel(lr_ref, li_ref, ldt_ref, lrc_ref, lic_ref, ldtc_ref, b16r_ref, b16i_ref, brr_ref,
                    bri_ref, crr_ref, cri_ref, ktoe_ref, bdr_ref, bdi_ref, cdr_ref, cdi_ref, al_ref,
                    *, chunk):
    hg = S5_GROUP
    n = chunk * hg
    lr, li = lr_ref[...], li_ref[...]
    dt = jnp.exp(ldt_ref[...])
    mag = jnp.exp(lr * dt)
    a_re = mag * jnp.cos(li * dt)
    a_im = mag * jnp.sin(li * dt)
    den = lr * lr + li * li
    z_re = ((a_re - 1.0) * lr + a_im * li) / den
    z_im = (a_im * lr - (a_re - 1.0) * li) / den

    def zoh(br, bi):
        return z_re * br - z_im * bi, z_re * bi + z_im * br

    e = (chunk - 1.0) - lax.broadcasted_iota(jnp.int32, (chunk, S5_STATE), 0).astype(F32)
    pm = jnp.exp(lr * dt * e)
    pr = jnp.concatenate([pm * jnp.cos(li * dt * e)] * hg, axis=0)
    pi = jnp.concatenate([pm * jnp.sin(li * dt * e)] * hg, axis=0)
    bb_re, bb_im = zoh(brr_ref[...], bri_ref[...])
    bdr_ref[...] = (bb_re * pr - bb_im * pi).astype(bdr_ref.dtype)
    bdi_ref[...] = (bb_re * pi + bb_im * pr).astype(bdi_ref.dtype)
    lm = jnp.exp(lr * dt * chunk)
    al_ref[0:1, :] = lm * jnp.cos(li * dt * chunk)
    al_ref[1:2, :] = lm * jnp.sin(li * dt * chunk)

    lrc, lic = lrc_ref[...], lic_ref[...]
    dtc = jnp.exp(ldtc_ref[...])
    tau = lax.broadcasted_iota(jnp.int32, (S5_STATE, chunk), 1).astype(F32)
    vm = jnp.exp(lrc * dtc * tau)
    vc = jnp.concatenate([vm * jnp.cos(lic * dtc * tau)] * hg, axis=1)
    vs = jnp.concatenate([vm * jnp.sin(lic * dtc * tau)] * hg, axis=1)
    crr, cri = crr_ref[...], cri_ref[...]
    v_re = vc * crr - vs * cri
    v_im = vc * cri + vs * crr
    magc = jnp.exp(lrc * dtc)
    ac_re = magc * jnp.cos(lic * dtc)
    ac_im = magc * jnp.sin(lic * dtc)
    cdr_ref[...] = (v_re * ac_re - v_im * ac_im).astype(cdr_ref.dtype)
    cdi_ref[...] = (-(v_re * ac_im + v_im * ac_re)).astype(cdi_ref.dtype)

    b16_re, b16_im = zoh(b16r_ref[...], b16i_ref[...])
    p = _dot(b16_re, v_re, HIGHEST) - _dot(b16_im, v_im, HIGHEST)
    row = lax.broadcasted_iota(jnp.int32, (chunk, n), 0)
    keep = lax.broadcasted_iota(jnp.int32, (chunk, n), 1) % chunk >= row
    for hi in range(hg):
        blk = pltpu.roll(jnp.broadcast_to(p[hi:hi + 1, :], (chunk, n)), 0, 1, stride=1, stride_axis=0)
        ktoe_ref[hi * chunk:(hi + 1) * chunk, :] = jnp.where(keep, blk, 0.0).astype(ktoe_ref.dtype)


def _s5_main_kernel(u_ref, ktoe_ref, bdr_ref, bdi_ref, cdr_ref, cdi_ref, al_ref, d_ref, y_ref,
                    sre_ref, sim_ref, *, batch):
    r2 = u_ref.shape[0]
    per_seq = r2 // batch
    c = S5_CHUNK
    pieces = [u_ref[:, hi, :] for hi in range(S5_GROUP)]
    lhs = jnp.concatenate([jnp.concatenate([p[:, 0:c] for p in pieces], axis=1),
                           jnp.concatenate([p[:, c:] for p in pieces], axis=1)], axis=0).astype(BF16)
    sre_ref[...] = _dot(lhs, bdr_ref[...])
    sim_ref[...] = _dot(lhs, bdi_ref[...])
    a_re = al_ref[0:1, :]
    a_im = al_ref[1:2, :]
    s_re = jnp.zeros((batch, S5_STATE), F32)
    s_im = jnp.zeros((batch, S5_STATE), F32)
    for i in range(per_seq):
        for base in (0, r2):
            rows = pl.ds(base + i, batch, stride=per_seq)
            e_re, e_im = sre_ref[rows, :], sim_ref[rows, :]
            sre_ref[rows, :] = s_re
            sim_ref[rows, :] = s_im
            s_re, s_im = a_re * s_re - a_im * s_im + e_re, a_re * s_im + a_im * s_re + e_im
    y = _dot(lhs, ktoe_ref[...])
    y += _dot(sre_ref[...].astype(BF16), cdr_ref[...])
    y += _dot(sim_ref[...].astype(BF16), cdi_ref[...])
    for ho in range(S5_GROUP):
        cols = slice(ho * c, (ho + 1) * c)
        y_ref[:, ho, :] = (jnp.concatenate([y[0:r2, cols], y[r2:, cols]], axis=1)
                           + d_ref[ho:ho + 1, :] * pieces[ho])


def s5_ssm(u3, lam_re, lam_im, b_re, b_im, c_re, c_im, d, log_dt, batch):
    r2, width, lanes = u3.shape
    g, hg, p, chunk = width // S5_GROUP, S5_GROUP, S5_STATE, S5_CHUNK
    assert lanes == LANES == 2 * chunk and r2 % batch == 0
    n = chunk * hg

    def grp(shape):
        return pl.BlockSpec((None,) + shape, lambda i: (i, 0, 0))

    ldt = jnp.broadcast_to(log_dt[:, None, None], (g, 1, p))
    prep_in = [lam_re[:, None, :], lam_im[:, None, :], ldt,
               lam_re[:, :, None], lam_im[:, :, None], jnp.swapaxes(ldt, 1, 2),
               b_re, b_im, jnp.repeat(b_re, chunk, axis=1), jnp.repeat(b_im, chunk, axis=1),
               jnp.repeat(c_re, chunk, axis=2), jnp.repeat(c_im, chunk, axis=2)]
    prep_specs = ([grp((1, p))] * 3 + [grp((p, 1))] * 3 + [grp((hg, p))] * 2 + [grp((n, p))] * 2
                  + [grp((p, n))] * 2)
    ktoe, bdr, bdi, cdr, cdi, al = pl.pallas_call(
        functools.partial(_s5_prep_kernel, chunk=chunk),
        grid=(g,), in_specs=prep_specs,
        out_specs=[grp((n, n)), grp((n, p)), grp((n, p)), grp((p, n)), grp((p, n)), grp((2, p))],
        out_shape=[jax.ShapeDtypeStruct((g, n, n), BF16),
                   jax.ShapeDtypeStruct((g, n, p), BF16), jax.ShapeDtypeStruct((g, n, p), BF16),
                   jax.ShapeDtypeStruct((g, p, n), BF16), jax.ShapeDtypeStruct((g, p, n), BF16),
                   jax.ShapeDtypeStruct((g, 2, p), F32)],
        compiler_params=_params("parallel"), name="s5_prep",
    )(*prep_in)

    channels = pl.BlockSpec((r2, hg, LANES), lambda i: (0, i, 0))
    dvec = jnp.broadcast_to(d[:, :, None], (g, hg, LANES))
    return pl.pallas_call(
        functools.partial(_s5_main_kernel, batch=batch),
        grid=(g,),
        in_specs=[channels, grp((n, n)), grp((n, p)), grp((n, p)), grp((p, n)), grp((p, n)),
                  grp((2, p)), grp((hg, LANES))],
        out_specs=channels,
        out_shape=jax.ShapeDtypeStruct(u3.shape, F32),
        scratch_shapes=[pltpu.VMEM((2 * r2, p), F32), pltpu.VMEM((2 * r2, p), F32)],
        compiler_params=_params("parallel"), name="s5_main",
    )(u3, ktoe, bdr, bdi, cdr, cdi, al, dvec)


def _mix0_out_kernel(o_ref, y3_ref, x_ref, wglu_ref, bglu_ref, wout_ref, out_ref):
    y = jnp.concatenate([y3_ref[k].T for k in range(y3_ref.shape[0])], axis=0)
    y = 0.5 * y * (1.0 + jnp.tanh(math.sqrt(2.0 / math.pi) * (y + 0.044715 * (y * y * y))))
    y = y * jax.nn.sigmoid(_dot(y.astype(BF16), wglu_ref[...]) + bglu_ref[...])
    merged = jnp.concatenate([o_ref[...].astype(BF16), y.astype(BF16)], axis=-1)
    out_ref[...] = x_ref[...] + _dot(merged, wout_ref[...])


def mix0_out(o, y3, x, w_glu, b_glu, w_out, tm):
    t, d = x.shape
    half = o.shape[1]
    row = lambda w: pl.BlockSpec((tm, w), lambda i: (i, 0))
    return pl.pallas_call(
        _mix0_out_kernel, grid=(t // tm,),
        in_specs=[row(half), pl.BlockSpec((tm // LANES, half, LANES), lambda i: (i, 0, 0)), row(d),
                  _resident(w_glu.shape), _resident((1, half)), _resident(w_out.shape)],
        out_specs=row(d), out_shape=jax.ShapeDtypeStruct((t, d), F32),
        compiler_params=_params("parallel"), name="mix0_out",
    )(o, y3, x, w_glu, b_glu.reshape(1, half), w_out)


def _xattn_kernel(x_ref, g_ref, wq_ref, kv_ref, wo_ref, o_ref):
    x = x_ref[...]
    d = x.shape[1]
    dh = d // XA_HEADS
    xn = _rms(x, g_ref[...]).astype(BF16)
    q = _dot(xn, wq_ref[...]).astype(BF16)
    heads = []
    for h in range(XA_HEADS):
        kh = kv_ref[:, pl.ds(h * dh, dh)]
        vh = kv_ref[:, pl.ds(d + h * dh, dh)]
        s = _dot_nt(q[:, h * dh:(h + 1) * dh], kh) * (dh ** -0.5)
        e = jnp.exp(s - jnp.max(s, axis=-1, keepdims=True))
        p = e * (1.0 / jnp.sum(e, axis=-1, keepdims=True))
        heads.append(_dot(p.astype(BF16), vh).astype(BF16))
    o_ref[...] = x + _dot(jnp.concatenate(heads, axis=-1), wo_ref[...])


def xattn(x, kv, gain, wq, wo, batch, seq, tm):
    t, d = x.shape
    n_mem = kv.shape[0] // batch
    ns = seq // tm
    row = pl.BlockSpec((tm, d), lambda b, s: (b * ns + s, 0))
    return pl.pallas_call(
        _xattn_kernel, grid=(batch, ns),
        in_specs=[row, _resident((1, d)), _resident(wq.shape),
                  pl.BlockSpec((n_mem, 2 * d), lambda b, s: (b, 0)), _resident(wo.shape)],
        out_specs=row, out_shape=jax.ShapeDtypeStruct((t, d), F32),
        compiler_params=_params("parallel", "parallel"), name="xattn",
    )(x, gain.reshape(1, d), wq, kv, wo)


def _ffn_kernel(x_ref, g_ref, wup_ref, cw_ref, wd_ref, fg_ref, o_ref, xn_ref, tail_ref, act_ref,
                *, final_norm):
    tm = x_ref.shape[0]
    f = wd_ref.shape[0]
    fc = FFN_COLS
    pad = SUBLANES

    @pl.when(pl.program_id(1) == 0)
    def _():
        tail_ref[...] = jnp.zeros_like(tail_ref)

    xn_ref[...] = _rms(x_ref[...], g_ref[...]).astype(BF16)

    def proj_conv(col0):
        cols = pl.ds(col0, fc)
        cur = _dot(xn_ref[...], wup_ref[:, cols])
        z = jnp.concatenate([tail_ref[:, cols], cur], axis=0)
        tail_ref[:, cols] = cur[tm - pad:, :]
        y = cw_ref[0:1, cols] * z
        y = cw_ref[1:2, cols] * z + pltpu.roll(y, 1, 0)
        y = cw_ref[2:3, cols] * z + pltpu.roll(y, 1, 0)
        return y[pad:, :]

    for cb in range(f // fc):
        up = proj_conv(cb * fc)
        gate = proj_conv(f + cb * fc)
        act_ref[:, pl.ds(cb * fc, fc)] = (_silu(gate) * up).astype(BF16)
    y = x_ref[...] + _dot(act_ref[...], wd_ref[...])
    if final_norm:
        y = _rms(y, fg_ref[...])
    o_ref[...] = y


def conv_ffn(x, gain, w_up, conv_w, w_down, final_gain, batch, seq, tm):
    t, d = x.shape
    f = w_down.shape[0]
    assert FFN_CONV == conv_w.shape[0] == 3 and f % FFN_COLS == 0
    ns = seq // tm
    final_norm = final_gain is not None
    fg = (final_gain if final_norm else jnp.ones((d,), F32)).reshape(1, d)
    row = pl.BlockSpec((tm, d), lambda b, s: (b * ns + s, 0))
    return pl.pallas_call(
        functools.partial(_ffn_kernel, final_norm=final_norm), grid=(batch, ns),
        in_specs=[row, _resident((1, d)), _resident(w_up.shape), _resident(conv_w.shape),
                  _resident(w_down.shape), _resident((1, d))],
        out_specs=row, out_shape=jax.ShapeDtypeStruct((t, d), F32),
        scratch_shapes=[pltpu.VMEM((tm, d), BF16), pltpu.VMEM((SUBLANES, 2 * f), F32),
                        pltpu.VMEM((tm, f), BF16)],
        compiler_params=_params("parallel", "arbitrary"), name="conv_ffn",
    )(x, gain.reshape(1, d), w_up, conv_w, w_down, fg)


def _unit_lower_inverses(a_list, eye):
    n = eye.shape[0]
    ms = [eye - a for a in a_list]
    ps = [(-a).astype(BF16) for a in a_list]
    ps = [_dot(p, p).astype(BF16) for p in ps]
    for _ in range(4):
        both = [_dot(jnp.concatenate([m.astype(BF16), p], axis=0), p) for m, p in zip(ms, ps)]
        ms = [m + b[0:n] for m, b in zip(ms, both)]
        ps = [b[n:].astype(BF16) for b in both]
    return [m + _dot(m.astype(BF16), p) for m, p in zip(ms, ps)]


def _gdn_proj_kernel(x_ref, g_ref, wqkv_ref, wz_ref, wg_ref, cw_ref, qkv_ref, z_ref, gates_ref,
                     xn_ref, tail_ref):
    tm = x_ref.shape[0]
    width = GDN_HEADS * GDN_HEAD_DIM
    dh = GDN_HEAD_DIM
    cb_cols = 2 * dh
    pad = SUBLANES

    @pl.when(pl.program_id(1) == 0)
    def _():
        tail_ref[...] = jnp.zeros_like(tail_ref)

    xn_ref[...] = _rms(x_ref[...], g_ref[...]).astype(BF16)
    for cb in range(3 * width // cb_cols):
        cols = pl.ds(cb * cb_cols, cb_cols)
        cur = _dot(xn_ref[...], wqkv_ref[:, cols])
        zz = jnp.concatenate([tail_ref[:, cols], cur], axis=0)
        tail_ref[:, cols] = cur[tm - pad:, :]
        zs = pltpu.roll(zz, 1, 0)
        near = cw_ref[3:4, cols] * zz + cw_ref[2:3, cols] * zs
        far = cw_ref[1:2, cols] * zz + cw_ref[0:1, cols] * zs
        y = _silu((near + pltpu.roll(far, 2, 0))[pad:, :])
        if cb * cb_cols < 2 * width:
            scale = dh ** -0.5 if cb * cb_cols < width else 1.0
            parts = []
            for j in range(cb_cols // dh):
                yh = y[:, j * dh:(j + 1) * dh]
                parts.append(yh * (lax.rsqrt(jnp.sum(yh * yh, axis=-1, keepdims=True) + EPS) * scale))
            y = jnp.concatenate(parts, axis=1)
        qkv_ref[:, cols] = y
    z_ref[...] = _dot(xn_ref[...], wz_ref[...])
    gates_ref[...] = _dot(xn_ref[...], wg_ref[...])


def gdn_proj(x, gain, w_qkv, w_z, w_gates, conv_w, batch, seq, tm):
    t, d = x.shape
    ns = seq // tm
    assert conv_w.shape[0] == GDN_CONV

    def row(w):
        return pl.BlockSpec((tm, w), lambda b, s: (b * ns + s, 0))

    widths = [w_qkv.shape[1], w_z.shape[1], w_gates.shape[1]]
    return pl.pallas_call(
        _gdn_proj_kernel, grid=(batch, ns),
        in_specs=[row(d), _resident((1, d)), _resident(w_qkv.shape), _resident(w_z.shape),
                  _resident(w_gates.shape), _resident(conv_w.shape)],
        out_specs=[row(w) for w in widths],
        out_shape=[jax.ShapeDtypeStruct((t, w), F32) for w in widths],
        scratch_shapes=[pltpu.VMEM((tm, d), BF16), pltpu.VMEM((SUBLANES, w_qkv.shape[1]), F32)],
        compiler_params=_params("parallel", "arbitrary"), name="gdn_proj",
    )(x, gain.reshape(1, d), w_qkv, w_z, w_gates, conv_w)


def _gdn_kernel(qkv_ref, z_ref, b_ref, a_ref, x_ref, alog_ref, dtb_ref, on_ref, wout_ref, o_ref,
                q_ref, k_ref, v_ref, w_ref, qk_ref, gc_ref, gct_ref, beta_ref, ob_ref, state_ref):
    ts = qkv_ref.shape[0]
    width = GDN_HEADS * GDN_HEAD_DIM
    dh = GDN_HEAD_DIM
    c = GDN_CHUNK
    heads = range(GDN_HEADS)

    @pl.when(pl.program_id(1) == 0)
    def _():
        state_ref[...] = jnp.zeros_like(state_ref)

    beta_ref[...] = jax.nn.sigmoid(b_ref[...])
    ag = a_ref[...] + dtb_ref[...]
    softplus = jnp.maximum(ag, 0.0) + jnp.log1p(jnp.exp(-jnp.abs(ag)))
    gc = -(jnp.exp(alog_ref[...]) * softplus)
    pos = lax.broadcasted_iota(jnp.int32, (ts, LANES), 0) % c
    shift = 1
    while shift < c:
        gc = gc + jnp.where(pos >= shift, pltpu.roll(gc, shift, 0), 0.0)
        shift *= 2
    gc_ref[...] = gc
    gct_ref[...] = gc.T

    ri = lax.broadcasted_iota(jnp.int32, (2 * c, 2 * c), 0)
    ci = lax.broadcasted_iota(jnp.int32, (2 * c, 2 * c), 1)
    same = (ri // c) == (ci // c)
    incl = same & (ri >= ci)
    strict = same & (ri > ci)
    eye = jnp.where(ri == ci, 1.0, 0.0).astype(F32)

    def hcols(h):
        return pl.ds(h * dh, dh)

    def wy_factors(dc):
        r0 = dc * 2 * c
        rows = pl.ds(r0, 2 * c)
        gcd = gc_ref[rows, :]
        gctd = gct_ref[:, rows]
        bet = beta_ref[rows, :]
        qs = [qkv_ref[rows, hcols(h)] for h in heads]
        ks = [qkv_ref[rows, hcols(GDN_HEADS + h)] for h in heads]
        vs = [qkv_ref[rows, hcols(2 * GDN_HEADS + h)] for h in heads]
        gcols = [gcd[:, h:h + 1] for h in heads]
        bcols = [bet[:, h:h + 1] for h in heads]
        decays = [jnp.where(incl, jnp.exp(jnp.where(incl, gcols[h] - gctd[h:h + 1, :], 0.0)), 0.0)
                  for h in heads]
        kbs = [ks[h] * bcols[h] for h in heads]
        kbf = [k.astype(BF16) for k in ks]
        a_mats = [jnp.where(strict, _dot_nt(kbs[h].astype(BF16), kbf[h]) * decays[h], 0.0) for h in heads]
        for h in heads:
            qk = jnp.where(incl, _dot_nt(qs[h].astype(BF16), kbf[h]) * decays[h], 0.0)
            qk_ref[rows, hcols(h)] = qk.astype(BF16)
        t_mats = [t.astype(BF16) for t in _unit_lower_inverses(a_mats, eye)]
        egs = [jnp.exp(g) for g in gcols]
        for h in heads:
            w_ref[rows, hcols(h)] = _dot(t_mats[h], (kbs[h] * egs[h]).astype(BF16)).astype(BF16)
        for h in heads:
            v_ref[rows, hcols(h)] = _dot(t_mats[h], (vs[h] * bcols[h]).astype(BF16))
        for h in heads:
            q_ref[rows, hcols(h)] = (qs[h] * egs[h]).astype(BF16)
            gl0 = gcd[c - 1:c, h:h + 1]
            gl1 = gcd[2 * c - 1:2 * c, h:h + 1]
            glcol = jnp.concatenate([jnp.broadcast_to(gl0, (c, 1)), jnp.broadcast_to(gl1, (c, 1))], axis=0)
            k_ref[rows, hcols(h)] = (ks[h] * jnp.exp(glcol - gcols[h])).astype(BF16)


    def recurrence(dc):
        r0 = dc * 2 * c
        rows = pl.ds(r0, 2 * c)
        sts = [state_ref[h] for h in heads]
        vns, outs = [], []
        for half in range(2):
            r = pl.ds(r0 + half * c, c)
            egl = jnp.exp(gc_ref[pl.ds(r0 + half * c + c - 1, 1), :])
            sbs = [s.astype(BF16) for s in sts]
            vn = [v_ref[r, hcols(h)] - _dot(w_ref[r, hcols(h)], sbs[h]) for h in heads]
            outs.append([_dot(q_ref[r, hcols(h)], sbs[h]) for h in heads])
            sts = [sts[h] * egl[:, h:h + 1] + _dot_tn(k_ref[r, hcols(h)], vn[h].astype(BF16))
                   for h in heads]
            vns.append(vn)
        for h in heads:
            state_ref[h] = sts[h]
        for h in heads:
            vn = jnp.concatenate([vns[0][h], vns[1][h]], axis=0).astype(BF16)
            o = jnp.concatenate([outs[0][h], outs[1][h]], axis=0) + _dot(qk_ref[rows, hcols(h)], vn)
            o = o * lax.rsqrt(jnp.mean(o * o, axis=-1, keepdims=True) + EPS) * on_ref[...]
            ob_ref[rows, hcols(h)] = (o * _silu(z_ref[rows, hcols(h)])).astype(BF16)

    n_pairs = ts // (2 * c)
    wy_factors(0)
    for dc in range(n_pairs):
        if dc + 1 < n_pairs:
            wy_factors(dc + 1)
        recurrence(dc)
    o_ref[...] = x_ref[...] + _dot(ob_ref[...], wout_ref[...])


def gated_deltanet(qkv, z, gates, x, a_log, dt_bias, o_norm, w_out, batch, seq, ts):
    t, d = x.shape
    width = GDN_HEADS * GDN_HEAD_DIM
    ns = seq // ts

    def lane_pad(vec):
        return jnp.pad(vec, (0, LANES - vec.shape[0])).reshape(1, LANES)

    def row(w, j):
        return pl.BlockSpec((ts, w), lambda b, s: (b * ns + s, j))

    return pl.pallas_call(
        _gdn_kernel, grid=(batch, ns),
        in_specs=[row(3 * width, 0), row(width, 0), row(LANES, 0), row(LANES, 1), row(d, 0),
                  _resident((1, LANES)), _resident((1, LANES)), _resident((1, GDN_HEAD_DIM)),
                  _resident(w_out.shape)],
        out_specs=row(d, 0), out_shape=jax.ShapeDtypeStruct((t, d), F32),
        scratch_shapes=[pltpu.VMEM((ts, width), BF16), pltpu.VMEM((ts, width), BF16),
                        pltpu.VMEM((ts, width), F32),
                        pltpu.VMEM((ts, width), BF16), pltpu.VMEM((ts, width), BF16),
                        pltpu.VMEM((ts, LANES), F32), pltpu.VMEM((LANES, ts), F32),
                        pltpu.VMEM((ts, LANES), F32), pltpu.VMEM((ts, width), BF16),
                        pltpu.VMEM((GDN_HEADS, GDN_HEAD_DIM, GDN_HEAD_DIM), F32)],
        compiler_params=_params("parallel", "arbitrary"), name="gated_deltanet",
    )(qkv, z, gates, gates, x, lane_pad(a_log), lane_pad(dt_bias), o_norm.reshape(1, GDN_HEAD_DIM), w_out)


def _common_block(x, mem, xa_norm, mem_norm, xa_wq, xa_wkv, xa_wo, ffn_norm, ffn_w_up, ffn_conv,
                  ffn_w_down, final_gain, batch, seq, tm):
    (kv,) = norm_proj(mem, mem_norm, [xa_wkv.astype(BF16)], [BF16], tm=min(tm, mem.shape[0]))
    x = xattn(x, kv, xa_norm, xa_wq.astype(BF16), xa_wo.astype(BF16), batch, seq, tm)
    return conv_ffn(x, ffn_norm, ffn_w_up.astype(BF16), ffn_conv, ffn_w_down.astype(BF16), final_gain,
                    batch, seq, tm)


def kernel(x, mem, l0_mix_norm, l0_w_in, l0_ret_norm, l0_s5_lambda_re, l0_s5_lambda_im, l0_s5_b_re, l0_s5_b_im, l0_s5_c_re, l0_s5_c_im, l0_s5_d, l0_s5_log_dt, l0_s5_w_glu, l0_s5_b_glu, l0_w_out, l0_xa_norm, l0_mem_norm, l0_xa_wq, l0_xa_wkv, l0_xa_wo, l0_ffn_norm, l0_ffn_w_up, l0_ffn_conv, l0_ffn_w_down, l1_mix_norm, l1_w_in, l1_conv, l1_a_log, l1_dt_bias, l1_o_norm, l1_w_out, l1_xa_norm, l1_mem_norm, l1_xa_wq, l1_xa_wkv, l1_xa_wo, l1_ffn_norm, l1_ffn_w_up, l1_ffn_conv, l1_ffn_w_down, final_norm):
    batch, seq, d = x.shape
    tm = min(512, seq)
    xf = x.reshape(batch * seq, d)
    memf = mem.reshape(-1, d)
    ret_width = RET_HEADS * RET_HEAD_DIM

    proj, u3 = even_proj(xf, l0_mix_norm, l0_w_in[:, :4 * ret_width].astype(BF16),
                         l0_w_in[:, 4 * ret_width:].T.astype(BF16), tm)
    o_ret = retention(proj, l0_ret_norm, batch, seq, tm)
    y3 = s5_ssm(u3, l0_s5_lambda_re, l0_s5_lambda_im, l0_s5_b_re, l0_s5_b_im, l0_s5_c_re, l0_s5_c_im,
                l0_s5_d, l0_s5_log_dt, batch)
    xf = mix0_out(o_ret, y3, xf, l0_s5_w_glu.astype(BF16), l0_s5_b_glu, l0_w_out.astype(BF16), tm)
    xf = _common_block(xf, memf, l0_xa_norm, l0_mem_norm, l0_xa_wq, l0_xa_wkv, l0_xa_wo, l0_ffn_norm,
                       l0_ffn_w_up, l0_ffn_conv, l0_ffn_w_down, None, batch, seq, tm)

    gdn_width = GDN_HEADS * GDN_HEAD_DIM
    w_qkv = l1_w_in[:, :3 * gdn_width].astype(BF16)
    w_z = l1_w_in[:, 3 * gdn_width:4 * gdn_width].astype(BF16)
    w_b = jnp.pad(l1_w_in[:, 4 * gdn_width:4 * gdn_width + GDN_HEADS], ((0, 0), (0, LANES - GDN_HEADS)))
    w_a = jnp.pad(l1_w_in[:, 4 * gdn_width + GDN_HEADS:], ((0, 0), (0, LANES - GDN_HEADS)))
    w_gates = jnp.concatenate([w_b, w_a], axis=1).astype(BF16)
    qkv, z, gates = gdn_proj(xf, l1_mix_norm, w_qkv, w_z, w_gates, l1_conv, batch, seq, tm)
    xf = gated_deltanet(qkv, z, gates, xf, l1_a_log, l1_dt_bias, l1_o_norm, l1_w_out.astype(BF16),
                        batch, seq, tm)
    xf = _common_block(xf, memf, l1_xa_norm, l1_mem_norm, l1_xa_wq, l1_xa_wkv, l1_xa_wo, l1_ffn_norm,
                       l1_ffn_w_up, l1_ffn_conv, l1_ffn_w_down, final_norm, batch, seq, tm)
    return xf.reshape(batch, seq, d)
```

```python
import functools
import math

import jax
import jax.numpy as jnp
from jax import lax
from jax.experimental import pallas as pl
from jax.experimental.pallas import tpu as pltpu

F32 = jnp.float32
BF16 = jnp.bfloat16
EPS = 1e-6

V7X_VMEM_BYTES = 64 * 1024 * 1024
VMEM_LIMIT_BYTES = V7X_VMEM_BYTES - 8 * 1024 * 1024
LANES = 128
SUBLANES = 8

RET_HEADS = 4
RET_HEAD_DIM = 128
RET_CHUNK = 128
ROPE_BASE = 10000.0
S5_GROUP = 16
S5_STATE = 64
S5_CHUNK = 64
GDN_HEADS = 8
GDN_HEAD_DIM = 128
GDN_CONV = 4
GDN_CHUNK = 64
XA_HEADS = 4
FFN_CONV = 3
FFN_COLS = 256

HIGHEST = lax.Precision.HIGHEST


def _params(*semantics):
    return pltpu.CompilerParams(dimension_semantics=semantics, vmem_limit_bytes=VMEM_LIMIT_BYTES)


def _resident(shape):
    nd = len(shape)
    return pl.BlockSpec(shape, lambda *_: (0,) * nd)


def _dot(a, b, precision=None):
    return jnp.dot(a, b, preferred_element_type=F32, precision=precision)


def _dot_nt(a, b):
    return lax.dot_general(a, b, (((1,), (1,)), ((), ())), preferred_element_type=F32)


def _dot_tn(a, b):
    return lax.dot_general(a, b, (((0,), (0,)), ((), ())), preferred_element_type=F32)


def _rms(x, g):
    return x * lax.rsqrt(jnp.mean(x * x, axis=-1, keepdims=True) + EPS) * g


def _silu(x):
    return x * jax.nn.sigmoid(x)


def _norm_proj_kernel(x_ref, g_ref, *refs):
    n_out = len(refs) // 2
    xn = _rms(x_ref[...], g_ref[...]).astype(BF16)
    for w_ref, o_ref in zip(refs[:n_out], refs[n_out:]):
        o_ref[...] = _dot(xn, w_ref[...]).astype(o_ref.dtype)


def norm_proj(x, gain, weights, out_dtypes, tm):
    t, d = x.shape
    in_specs = [pl.BlockSpec((tm, d), lambda i: (i, 0)), _resident((1, d))]
    in_specs += [_resident(w.shape) for w in weights]
    out_specs = [pl.BlockSpec((tm, w.shape[1]), lambda i: (i, 0)) for w in weights]
    out_shape = [jax.ShapeDtypeStruct((t, w.shape[1]), dt) for w, dt in zip(weights, out_dtypes)]
    return pl.pallas_call(
        _norm_proj_kernel, grid=(t // tm,), in_specs=in_specs, out_specs=out_specs,
        out_shape=out_shape, compiler_params=_params("parallel"), name="norm_proj",
    )(x, gain.reshape(1, d), *weights)


def _even_proj_kernel(x_ref, g_ref, w_ref, wut_ref, o_ref, u3_ref):
    xn = _rms(x_ref[...], g_ref[...]).astype(BF16)
    o_ref[...] = _dot(xn, w_ref[...]).astype(o_ref.dtype)
    ut = _dot_nt(wut_ref[...], xn)
    for k in range(u3_ref.shape[0]):
        u3_ref[k] = ut[:, k * LANES:(k + 1) * LANES]


def even_proj(x, gain, w, w_u_t, tm):
    t, d = x.shape
    n, nu = w.shape[1], w_u_t.shape[0]
    return pl.pallas_call(
        _even_proj_kernel, grid=(t // tm,),
        in_specs=[pl.BlockSpec((tm, d), lambda i: (i, 0)), _resident((1, d)), _resident(w.shape),
                  _resident(w_u_t.shape)],
        out_specs=[pl.BlockSpec((tm, n), lambda i: (i, 0)),
                   pl.BlockSpec((tm // LANES, nu, LANES), lambda i: (i, 0, 0))],
        out_shape=[jax.ShapeDtypeStruct((t, n), BF16), jax.ShapeDtypeStruct((t // LANES, nu, LANES), F32)],
        compiler_params=_params("parallel"), name="even_proj",
    )(x, gain.reshape(1, d), w, w_u_t)


def _retention_kernel(q_ref, k_ref, v_ref, gt_ref, cos_ref, sin_ref, intra_ref, qdec_ref, kdec_ref,
                      cdec_ref, rn_ref, o_ref, state_ref, *, n_chunks):
    c = RET_CHUNK

    @pl.when(pl.program_id(1) == 0)
    def _():
        state_ref[...] = jnp.zeros_like(state_ref)

    scale = RET_HEAD_DIM ** -0.5
    for ci in range(n_chunks):
        rows = pl.ds(ci * c, c)
        cos = cos_ref[rows, :]
        sin = sin_ref[rows, :]
        for h in range(RET_HEADS):
            cols = pl.ds(h * RET_HEAD_DIM, RET_HEAD_DIM)
            qh = q_ref[rows, cols].astype(F32)
            kh = k_ref[rows, cols].astype(F32)
            vh = v_ref[rows, cols]
            qr = qh * cos + pltpu.roll(qh, RET_HEAD_DIM // 2, 1) * sin
            kr = (kh * cos + pltpu.roll(kh, RET_HEAD_DIM // 2, 1) * sin) * scale
            scores = _dot_nt(qr.astype(BF16), kr.astype(BF16)) * intra_ref[h]
            inner = _dot(scores.astype(BF16), vh)
            st = state_ref[h]
            cross = _dot((qr * qdec_ref[:, cols]).astype(BF16), st.astype(BF16))
            kv = _dot_tn((kr * kdec_ref[:, cols]).astype(BF16), vh)
            state_ref[h] = st * cdec_ref[:, cols] + kv
            o = inner + cross
            o = o * lax.rsqrt(jnp.mean(o * o, axis=-1, keepdims=True) + EPS)
            o_ref[rows, cols] = (o * rn_ref[:, cols] * _silu(gt_ref[rows, cols].astype(F32))).astype(o_ref.dtype)


def retention(proj, ret_norm, batch, seq, ts):
    t = proj.shape[0]
    width = RET_HEADS * RET_HEAD_DIM
    c = RET_CHUNK
    half = RET_HEAD_DIM // 2
    ns = seq // ts
    inv = jnp.exp(-math.log(ROPE_BASE) * jnp.arange(half, dtype=F32) / half)
    ang = jnp.arange(seq, dtype=F32)[:, None] * inv[None, :]
    cos = jnp.concatenate([jnp.cos(ang), jnp.cos(ang)], axis=-1)
    sin = jnp.concatenate([-jnp.sin(ang), jnp.sin(ang)], axis=-1)
    log_gamma = jnp.log1p(-jnp.exp2(-5.0 - jnp.arange(RET_HEADS, dtype=F32)))
    idx = jnp.arange(c, dtype=F32)
    diff = idx[:, None] - idx[None, :]
    causal = diff >= 0
    intra = jnp.where(causal, jnp.exp(log_gamma[:, None, None] * jnp.where(causal, diff, 0.0)), 0.0)

    def per_head_cols(tab):
        return jnp.repeat(tab.T, RET_HEAD_DIM, axis=1)

    qdec = per_head_cols(jnp.exp(log_gamma[:, None] * (idx + 1)))
    kdec = per_head_cols(jnp.exp(log_gamma[:, None] * (c - 1 - idx)))
    cdec = per_head_cols(jnp.exp(log_gamma * c)[:, None])

    def col_block(j):
        return pl.BlockSpec((ts, width), lambda b, s: (b * ns + s, j))

    in_specs = [col_block(0), col_block(1), col_block(2), col_block(3),
                pl.BlockSpec((ts, RET_HEAD_DIM), lambda b, s: (s, 0)),
                pl.BlockSpec((ts, RET_HEAD_DIM), lambda b, s: (s, 0)),
                _resident(intra.shape), _resident(qdec.shape), _resident(kdec.shape),
                _resident(cdec.shape), _resident((1, width))]
    return pl.pallas_call(
        functools.partial(_retention_kernel, n_chunks=ts // c),
        grid=(batch, ns), in_specs=in_specs,
        out_specs=pl.BlockSpec((ts, width), lambda b, s: (b * ns + s, 0)),
        out_shape=jax.ShapeDtypeStruct((t, width), BF16),
        scratch_shapes=[pltpu.VMEM((RET_HEADS, RET_HEAD_DIM, RET_HEAD_DIM), F32)],
        compiler_params=_params("parallel", "arbitrary"), name="retention",
    )(proj, proj, proj, proj, cos, sin, intra, qdec, kdec, cdec, ret_norm.reshape(1, width))


def _s5_prep_kernel(lr_ref, li_ref, ldt_ref, lrc_ref, lic_ref, ldtc_ref, b16r_ref, b16i_ref, brr_ref,
                    bri_ref, crr_ref, cri_ref, ktoe_ref, bdr_ref, bdi_ref, cdr_ref, cdi_ref, al_ref,
                    *, chunk):
    hg = S5_GROUP
    n = chunk * hg
    lr, li = lr_ref[...], li_ref[...]
    dt = jnp.exp(ldt_ref[...])
    mag = jnp.exp(lr * dt)
    a_re = mag * jnp.cos(li * dt)
    a_im = mag * jnp.sin(li * dt)
    den = lr * lr + li * li
    z_re = ((a_re - 1.0) * lr + a_im * li) / den
    z_im = (a_im * lr - (a_re - 1.0) * li) / den

    def zoh(br, bi):
        return z_re * br - z_im * bi, z_re * bi + z_im * br

    e = (chunk - 1.0) - lax.broadcasted_iota(jnp.int32, (chunk, S5_STATE), 0).astype(F32)
    pm = jnp.exp(lr * dt * e)
    pr = jnp.concatenate([pm * jnp.cos(li * dt * e)] * hg, axis=0)
    pi = jnp.concatenate([pm * jnp.sin(li * dt * e)] * hg, axis=0)
    bb_re, bb_im = zoh(brr_ref[...], bri_ref[...])
    bdr_ref[...] = (bb_re * pr - bb_im * pi).astype(bdr_ref.dtype)
    bdi_ref[...] = (bb_re * pi + bb_im * pr).astype(bdi_ref.dtype)
    lm = jnp.exp(lr * dt * chunk)
    al_ref[0:1, :] = lm * jnp.cos(li * dt * chunk)
    al_ref[1:2, :] = lm * jnp.sin(li * dt * chunk)

    lrc, lic = lrc_ref[...], lic_ref[...]
    dtc = jnp.exp(ldtc_ref[...])
    tau = lax.broadcasted_iota(jnp.int32, (S5_STATE, chunk), 1).astype(F32)
    vm = jnp.exp(lrc * dtc * tau)
    vc = jnp.concatenate([vm * jnp.cos(lic * dtc * tau)] * hg, axis=1)
    vs = jnp.concatenate([vm * jnp.sin(lic * dtc * tau)] * hg, axis=1)
    crr, cri = crr_ref[...], cri_ref[...]
    v_re = vc * crr - vs * cri
    v_im = vc * cri + vs * crr
    magc = jnp.exp(lrc * dtc)
    ac_re = magc * jnp.cos(lic * dtc)
    ac_im = magc * jnp.sin(lic * dtc)
    cdr_ref[...] = (v_re * ac_re - v_im * ac_im).astype(cdr_ref.dtype)
    cdi_ref[...] = (-(v_re * ac_im + v_im * ac_re)).astype(cdi_ref.dtype)

    b16_re, b16_im = zoh(b16r_ref[...], b16i_ref[...])
    p = _dot(b16_re, v_re, HIGHEST) - _dot(b16_im, v_im, HIGHEST)
    row = lax.broadcasted_iota(jnp.int32, (chunk, n), 0)
    keep = lax.broadcasted_iota(jnp.int32, (chunk, n), 1) % chunk >= row
    for hi in range(hg):
        blk = pltpu.roll(jnp.broadcast_to(p[hi:hi + 1, :], (chunk, n)), 0, 1, stride=1, stride_axis=0)
        ktoe_ref[hi * chunk:(hi + 1) * chunk, :] = jnp.where(keep, blk, 0.0).astype(ktoe_ref.dtype)


def _s5_main_kernel(u_ref, ktoe_ref, bdr_ref, bdi_ref, cdr_ref, cdi_ref, al_ref, d_ref, y_ref,
                    sre_ref, sim_ref, *, batch):
    r2 = u_ref.shape[0]
    per_seq = r2 // batch
    c = S5_CHUNK
    pieces = [u_ref[:, hi, :] for hi in range(S5_GROUP)]
    lhs = jnp.concatenate([jnp.concatenate([p[:, 0:c] for p in pieces], axis=1),
                           jnp.concatenate([p[:, c:] for p in pieces], axis=1)], axis=0).astype(BF16)
    sre_ref[...] = _dot(lhs, bdr_ref[...])
    sim_ref[...] = _dot(lhs, bdi_ref[...])
    a_re = al_ref[0:1, :]
    a_im = al_ref[1:2, :]
    s_re = jnp.zeros((batch, S5_STATE), F32)
    s_im = jnp.zeros((batch, S5_STATE), F32)
    for i in range(per_seq):
        for base in (0, r2):
            rows = pl.ds(base + i, batch, stride=per_seq)
            e_re, e_im = sre_ref[rows, :], sim_ref[rows, :]
            sre_ref[rows, :] = s_re
            sim_ref[rows, :] = s_im
            s_re, s_im = a_re * s_re - a_im * s_im + e_re, a_re * s_im + a_im * s_re + e_im
    y = _dot(lhs, ktoe_ref[...])
    y += _dot(sre_ref[...].astype(BF16), cdr_ref[...])
    y += _dot(sim_ref[...].astype(BF16), cdi_ref[...])
    for ho in range(S5_GROUP):
        cols = slice(ho * c, (ho + 1) * c)
        y_ref[:, ho, :] = (jnp.concatenate([y[0:r2, cols], y[r2:, cols]], axis=1)
                           + d_ref[ho:ho + 1, :] * pieces[ho])


def s5_ssm(u3, lam_re, lam_im, b_re, b_im, c_re, c_im, d, log_dt, batch):
    r2, width, lanes = u3.shape
    g, hg, p, chunk = width // S5_GROUP, S5_GROUP, S5_STATE, S5_CHUNK
    assert lanes == LANES == 2 * chunk and r2 % batch == 0
    n = chunk * hg

    def grp(shape):
        return pl.BlockSpec((None,) + shape, lambda i: (i, 0, 0))

    ldt = jnp.broadcast_to(log_dt[:, None, None], (g, 1, p))
    prep_in = [lam_re[:, None, :], lam_im[:, None, :], ldt,
               lam_re[:, :, None], lam_im[:, :, None], jnp.swapaxes(ldt, 1, 2),
               b_re, b_im, jnp.repeat(b_re, chunk, axis=1), jnp.repeat(b_im, chunk, axis=1),
               jnp.repeat(c_re, chunk, axis=2), jnp.repeat(c_im, chunk, axis=2)]
    prep_specs = ([grp((1, p))] * 3 + [grp((p, 1))] * 3 + [grp((hg, p))] * 2 + [grp((n, p))] * 2
                  + [grp((p, n))] * 2)
    ktoe, bdr, bdi, cdr, cdi, al = pl.pallas_call(
        functools.partial(_s5_prep_kernel, chunk=chunk),
        grid=(g,), in_specs=prep_specs,
        out_specs=[grp((n, n)), grp((n, p)), grp((n, p)), grp((p, n)), grp((p, n)), grp((2, p))],
        out_shape=[jax.ShapeDtypeStruct((g, n, n), BF16),
                   jax.ShapeDtypeStruct((g, n, p), BF16), jax.ShapeDtypeStruct((g, n, p), BF16),
                   jax.ShapeDtypeStruct((g, p, n), BF16), jax.ShapeDtypeStruct((g, p, n), BF16),
                   jax.ShapeDtypeStruct((g, 2, p), F32)],
        compiler_params=_params("parallel"), name="s5_prep",
    )(*prep_in)

    channels = pl.BlockSpec((r2, hg, LANES), lambda i: (0, i, 0))
    dvec = jnp.broadcast_to(d[:, :, None], (g, hg, LANES))
    return pl.pallas_call(
        functools.partial(_s5_main_kernel, batch=batch),
        grid=(g,),
        in_specs=[channels, grp((n, n)), grp((n, p)), grp((n, p)), grp((p, n)), grp((p, n)),
                  grp((2, p)), grp((hg, LANES))],
        out_specs=channels,
        out_shape=jax.ShapeDtypeStruct(u3.shape, F32),
        scratch_shapes=[pltpu.VMEM((2 * r2, p), F32), pltpu.VMEM((2 * r2, p), F32)],
        compiler_params=_params("parallel"), name="s5_main",
    )(u3, ktoe, bdr, bdi, cdr, cdi, al, dvec)


def _mix0_out_kernel(o_ref, y3_ref, x_ref, wglu_ref, bglu_ref, wout_ref, out_ref):
    y = jnp.concatenate([y3_ref[k].T for k in range(y3_ref.shape[0])], axis=0)
    y = 0.5 * y * (1.0 + jnp.tanh(math.sqrt(2.0 / math.pi) * (y + 0.044715 * (y * y * y))))
    y = y * jax.nn.sigmoid(_dot(y.astype(BF16), wglu_ref[...]) + bglu_ref[...])
    merged = jnp.concatenate([o_ref[...], y.astype(BF16)], axis=-1)
    out_ref[...] = x_ref[...] + _dot(merged, wout_ref[...])


def mix0_out(o, y3, x, w_glu, b_glu, w_out, tm):
    t, d = x.shape
    half = o.shape[1]
    row = lambda w: pl.BlockSpec((tm, w), lambda i: (i, 0))
    return pl.pallas_call(
        _mix0_out_kernel, grid=(t // tm,),
        in_specs=[row(half), pl.BlockSpec((tm // LANES, half, LANES), lambda i: (i, 0, 0)), row(d),
                  _resident(w_glu.shape), _resident((1, half)), _resident(w_out.shape)],
        out_specs=row(d), out_shape=jax.ShapeDtypeStruct((t, d), F32),
        compiler_params=_params("parallel"), name="mix0_out",
    )(o, y3, x, w_glu, b_glu.reshape(1, half), w_out)


def _xattn_kernel(x_ref, g_ref, wq_ref, kv_ref, wo_ref, o_ref):
    x = x_ref[...]
    d = x.shape[1]
    dh = d // XA_HEADS
    xn = _rms(x, g_ref[...]).astype(BF16)
    q = _dot(xn, wq_ref[...]).astype(BF16)
    heads = []
    for h in range(XA_HEADS):
        kh = kv_ref[:, pl.ds(h * dh, dh)]
        vh = kv_ref[:, pl.ds(d + h * dh, dh)]
        s = _dot_nt(q[:, h * dh:(h + 1) * dh], kh) * (dh ** -0.5)
        e = jnp.exp(s - jnp.max(s, axis=-1, keepdims=True))
        p = e * (1.0 / jnp.sum(e, axis=-1, keepdims=True))
        heads.append(_dot(p.astype(BF16), vh).astype(BF16))
    o_ref[...] = x + _dot(jnp.concatenate(heads, axis=-1), wo_ref[...])


def xattn(x, kv, gain, wq, wo, batch, seq, tm):
    t, d = x.shape
    n_mem = kv.shape[0] // batch
    ns = seq // tm
    row = pl.BlockSpec((tm, d), lambda b, s: (b * ns + s, 0))
    return pl.pallas_call(
        _xattn_kernel, grid=(batch, ns),
        in_specs=[row, _resident((1, d)), _resident(wq.shape),
                  pl.BlockSpec((n_mem, 2 * d), lambda b, s: (b, 0)), _resident(wo.shape)],
        out_specs=row, out_shape=jax.ShapeDtypeStruct((t, d), F32),
        compiler_params=_params("parallel", "parallel"), name="xattn",
    )(x, gain.reshape(1, d), wq, kv, wo)


def _ffn_kernel(x_ref, g_ref, wup_ref, cw_ref, wd_ref, fg_ref, o_ref, xn_ref, tail_ref, act_ref,
                *, final_norm):
    tm = x_ref.shape[0]
    f = wd_ref.shape[0]
    fc = FFN_COLS
    pad = SUBLANES

    @pl.when(pl.program_id(1) == 0)
    def _():
        tail_ref[...] = jnp.zeros_like(tail_ref)

    xn_ref[...] = _rms(x_ref[...], g_ref[...]).astype(BF16)

    def proj_conv(col0):
        cols = pl.ds(col0, fc)
        cur = _dot(xn_ref[...], wup_ref[:, cols])
        z = jnp.concatenate([tail_ref[:, cols], cur], axis=0)
        tail_ref[:, cols] = cur[tm - pad:, :]
        y = cw_ref[0:1, cols] * z
        y = cw_ref[1:2, cols] * z + pltpu.roll(y, 1, 0)
        y = cw_ref[2:3, cols] * z + pltpu.roll(y, 1, 0)
        return y[pad:, :]

    for cb in range(f // fc):
        up = proj_conv(cb * fc)
        gate = proj_conv(f + cb * fc)
        act_ref[:, pl.ds(cb * fc, fc)] = (_silu(gate) * up).astype(BF16)
    y = x_ref[...] + _dot(act_ref[...], wd_ref[...])
    if final_norm:
        y = _rms(y, fg_ref[...])
    o_ref[...] = y


def conv_ffn(x, gain, w_up, conv_w, w_down, final_gain, batch, seq, tm):
    t, d = x.shape
    f = w_down.shape[0]
    assert FFN_CONV == conv_w.shape[0] == 3 and f % FFN_COLS == 0
    ns = seq // tm
    final_norm = final_gain is not None
    fg = (final_gain if final_norm else jnp.ones((d,), F32)).reshape(1, d)
    row = pl.BlockSpec((tm, d), lambda b, s: (b * ns + s, 0))
    return pl.pallas_call(
        functools.partial(_ffn_kernel, final_norm=final_norm), grid=(batch, ns),
        in_specs=[row, _resident((1, d)), _resident(w_up.shape), _resident(conv_w.shape),
                  _resident(w_down.shape), _resident((1, d))],
        out_specs=row, out_shape=jax.ShapeDtypeStruct((t, d), F32),
        scratch_shapes=[pltpu.VMEM((tm, d), BF16), pltpu.VMEM((SUBLANES, 2 * f), F32),
                        pltpu.VMEM((tm, f), BF16)],
        compiler_params=_params("parallel", "arbitrary"), name="conv_ffn",
    )(x, gain.reshape(1, d), w_up, conv_w, w_down, fg)


def _unit_lower_inverses(a_list, eye):
    n = eye.shape[0]
    ms = [eye - a for a in a_list]
    ps = [(-a).astype(BF16) for a in a_list]
    ps = [_dot(p, p).astype(BF16) for p in ps]
    for _ in range(4):
        both = [_dot(jnp.concatenate([m.astype(BF16), p], axis=0), p) for m, p in zip(ms, ps)]
        ms = [m + b[0:n] for m, b in zip(ms, both)]
        ps = [b[n:].astype(BF16) for b in both]
    return [m + _dot(m.astype(BF16), p) for m, p in zip(ms, ps)]


def _gdn_proj_kernel(x_ref, g_ref, wqkv_ref, wz_ref, wg_ref, cw_ref, qkv_ref, z_ref, gates_ref,
                     xn_ref, tail_ref):
    tm = x_ref.shape[0]
    width = GDN_HEADS * GDN_HEAD_DIM
    dh = GDN_HEAD_DIM
    cb_cols = 2 * dh
    pad = SUBLANES

    @pl.when(pl.program_id(1) == 0)
    def _():
        tail_ref[...] = jnp.zeros_like(tail_ref)

    xn_ref[...] = _rms(x_ref[...], g_ref[...]).astype(BF16)
    for cb in range(3 * width // cb_cols):
        cols = pl.ds(cb * cb_cols, cb_cols)
        cur = _dot(xn_ref[...], wqkv_ref[:, cols])
        zz = jnp.concatenate([tail_ref[:, cols], cur], axis=0)
        tail_ref[:, cols] = cur[tm - pad:, :]
        zs = pltpu.roll(zz, 1, 0)
        near = cw_ref[3:4, cols] * zz + cw_ref[2:3, cols] * zs
        far = cw_ref[1:2, cols] * zz + cw_ref[0:1, cols] * zs
        y = _silu((near + pltpu.roll(far, 2, 0))[pad:, :])
        if cb * cb_cols < 2 * width:
            scale = dh ** -0.5 if cb * cb_cols < width else 1.0
            parts = []
            for j in range(cb_cols // dh):
                yh = y[:, j * dh:(j + 1) * dh]
                parts.append(yh * (lax.rsqrt(jnp.sum(yh * yh, axis=-1, keepdims=True) + EPS) * scale))
            y = jnp.concatenate(parts, axis=1)
        qkv_ref[:, cols] = y.astype(qkv_ref.dtype)
        if cb % 3 == 2:
            zc = pl.ds((cb // 3) * cb_cols, cb_cols)
            z_ref[:, zc] = _dot(xn_ref[...], wz_ref[:, zc]).astype(z_ref.dtype)
        if cb == 0:
            gates_ref[...] = _dot(xn_ref[...], wg_ref[...])


def gdn_proj(x, gain, w_qkv, w_z, w_gates, conv_w, batch, seq, tm):
    t, d = x.shape
    ns = seq // tm
    assert conv_w.shape[0] == GDN_CONV

    def row(w):
        return pl.BlockSpec((tm, w), lambda b, s: (b * ns + s, 0))

    widths = [w_qkv.shape[1], w_z.shape[1], w_gates.shape[1]]
    return pl.pallas_call(
        _gdn_proj_kernel, grid=(batch, ns),
        in_specs=[row(d), _resident((1, d)), _resident(w_qkv.shape), _resident(w_z.shape),
                  _resident(w_gates.shape), _resident(conv_w.shape)],
        out_specs=[row(w) for w in widths],
        out_shape=[jax.ShapeDtypeStruct((t, w), dt) for w, dt in zip(widths, (BF16, BF16, F32))],
        scratch_shapes=[pltpu.VMEM((tm, d), BF16), pltpu.VMEM((SUBLANES, w_qkv.shape[1]), F32)],
        compiler_params=_params("parallel", "arbitrary"), name="gdn_proj",
    )(x, gain.reshape(1, d), w_qkv, w_z, w_gates, conv_w)


def _gdn_kernel(qkv_ref, z_ref, b_ref, a_ref, x_ref, alog_ref, dtb_ref, on_ref, wout_ref, o_ref,
                q_ref, k_ref, v_ref, w_ref, qk_ref, gc_ref, gct_ref, beta_ref, ob_ref, state_ref):
    ts = qkv_ref.shape[0]
    width = GDN_HEADS * GDN_HEAD_DIM
    dh = GDN_HEAD_DIM
    c = GDN_CHUNK
    heads = range(GDN_HEADS)

    @pl.when(pl.program_id(1) == 0)
    def _():
        state_ref[...] = jnp.zeros_like(state_ref)

    beta_ref[...] = jax.nn.sigmoid(b_ref[...])
    ag = a_ref[...] + dtb_ref[...]
    softplus = jnp.maximum(ag, 0.0) + jnp.log1p(jnp.exp(-jnp.abs(ag)))
    gc = -(jnp.exp(alog_ref[...]) * softplus)
    pos = lax.broadcasted_iota(jnp.int32, (ts, LANES), 0) % c
    shift = 1
    while shift < c:
        gc = gc + jnp.where(pos >= shift, pltpu.roll(gc, shift, 0), 0.0)
        shift *= 2
    gc_ref[...] = gc
    gct_ref[...] = gc.T

    ri = lax.broadcasted_iota(jnp.int32, (2 * c, 2 * c), 0)
    ci = lax.broadcasted_iota(jnp.int32, (2 * c, 2 * c), 1)
    same = (ri // c) == (ci // c)
    incl = same & (ri >= ci)
    strict = same & (ri > ci)
    eye = jnp.where(ri == ci, 1.0, 0.0).astype(F32)

    def hcols(h):
        return pl.ds(h * dh, dh)

    def wy_factors(dc):
        r0 = dc * 2 * c
        rows = pl.ds(r0, 2 * c)
        gcd = gc_ref[rows, :]
        gctd = gct_ref[:, rows]
        bet = beta_ref[rows, :]
        qbf = [qkv_ref[rows, hcols(h)] for h in heads]
        kbf = [qkv_ref[rows, hcols(GDN_HEADS + h)] for h in heads]
        qs = [q.astype(F32) for q in qbf]
        ks = [k.astype(F32) for k in kbf]
        vs = [qkv_ref[rows, hcols(2 * GDN_HEADS + h)].astype(F32) for h in heads]
        gcols = [gcd[:, h:h + 1] for h in heads]
        bcols = [bet[:, h:h + 1] for h in heads]
        decays = [jnp.where(incl, jnp.exp(jnp.where(incl, gcols[h] - gctd[h:h + 1, :], 0.0)), 0.0)
                  for h in heads]
        kbs = [ks[h] * bcols[h] for h in heads]
        both = [_dot_nt(jnp.concatenate([kbs[h].astype(BF16), qbf[h]], axis=0), kbf[h]) for h in heads]
        a_mats = [jnp.where(strict, both[h][0:2 * c] * decays[h], 0.0) for h in heads]
        for h in heads:
            qk_ref[rows, hcols(h)] = jnp.where(incl, both[h][2 * c:] * decays[h], 0.0).astype(BF16)
        t_mats = [t.astype(BF16) for t in _unit_lower_inverses(a_mats, eye)]
        egs = [jnp.exp(g) for g in gcols]
        for h in heads:
            w_ref[rows, hcols(h)] = _dot(t_mats[h], (kbs[h] * egs[h]).astype(BF16)).astype(BF16)
        for h in heads:
            v_ref[rows, hcols(h)] = _dot(t_mats[h], (vs[h] * bcols[h]).astype(BF16))
        for h in heads:
            q_ref[rows, hcols(h)] = (qs[h] * egs[h]).astype(BF16)
            gl0 = gcd[c - 1:c, h:h + 1]
            gl1 = gcd[2 * c - 1:2 * c, h:h + 1]
            glcol = jnp.concatenate([jnp.broadcast_to(gl0, (c, 1)), jnp.broadcast_to(gl1, (c, 1))], axis=0)
            k_ref[rows, hcols(h)] = (ks[h] * jnp.exp(glcol - gcols[h])).astype(BF16)


    def recurrence(dc):
        r0 = dc * 2 * c
        rows = pl.ds(r0, 2 * c)
        sts = [state_ref[h] for h in heads]
        vns, outs = [], []
        for half in range(2):
            r = pl.ds(r0 + half * c, c)
            egl = jnp.exp(gc_ref[pl.ds(r0 + half * c + c - 1, 1), :])
            sbs = [s.astype(BF16) for s in sts]
            ws = [_dot(jnp.concatenate([w_ref[r, hcols(h)], q_ref[r, hcols(h)]], axis=0), sbs[h]) for h in heads]
            vn = [v_ref[r, hcols(h)] - ws[h][0:c] for h in heads]
            outs.append([ws[h][c:] for h in heads])
            sts = [sts[h] * egl[:, h:h + 1] + _dot_tn(k_ref[r, hcols(h)], vn[h].astype(BF16))
                   for h in heads]
            vns.append(vn)
        for h in heads:
            state_ref[h] = sts[h]
        for h in heads:
            vn = jnp.concatenate([vns[0][h], vns[1][h]], axis=0).astype(BF16)
            o = jnp.concatenate([outs[0][h], outs[1][h]], axis=0) + _dot(qk_ref[rows, hcols(h)], vn)
            o = o * lax.rsqrt(jnp.mean(o * o, axis=-1, keepdims=True) + EPS) * on_ref[...]
            ob_ref[rows, hcols(h)] = (o * _silu(z_ref[rows, hcols(h)].astype(F32))).astype(BF16)

    n_pairs = ts // (2 * c)
    wy_factors(0)
    for dc in range(n_pairs):
        if dc + 1 < n_pairs:
            wy_factors(dc + 1)
        recurrence(dc)
    o_ref[...] = x_ref[...] + _dot(ob_ref[...], wout_ref[...])


def gated_deltanet(qkv, z, gates, x, a_log, dt_bias, o_norm, w_out, batch, seq, ts):
    t, d = x.shape
    width = GDN_HEADS * GDN_HEAD_DIM
    ns = seq // ts

    def lane_pad(vec):
        return jnp.pad(vec, (0, LANES - vec.shape[0])).reshape(1, LANES)

    def row(w, j):
        return pl.BlockSpec((ts, w), lambda b, s: (b * ns + s, j))

    return pl.pallas_call(
        _gdn_kernel, grid=(batch, ns),
        in_specs=[row(3 * width, 0), row(width, 0), row(LANES, 0), row(LANES, 1), row(d, 0),
                  _resident((1, LANES)), _resident((1, LANES)), _resident((1, GDN_HEAD_DIM)),
                  _resident(w_out.shape)],
        out_specs=row(d, 0), out_shape=jax.ShapeDtypeStruct((t, d), F32),
        scratch_shapes=[pltpu.VMEM((ts, width), BF16), pltpu.VMEM((ts, width), BF16),
                        pltpu.VMEM((ts, width), F32),
                        pltpu.VMEM((ts, width), BF16), pltpu.VMEM((ts, width), BF16),
                        pltpu.VMEM((ts, LANES), F32), pltpu.VMEM((LANES, ts), F32),
                        pltpu.VMEM((ts, LANES), F32), pltpu.VMEM((ts, width), BF16),
                        pltpu.VMEM((GDN_HEADS, GDN_HEAD_DIM, GDN_HEAD_DIM), F32)],
        compiler_params=_params("parallel", "arbitrary"), name="gated_deltanet",
    )(qkv, z, gates, gates, x, lane_pad(a_log), lane_pad(dt_bias), o_norm.reshape(1, GDN_HEAD_DIM), w_out)


def _common_block(x, mem, xa_norm, mem_norm, xa_wq, xa_wkv, xa_wo, ffn_norm, ffn_w_up, ffn_conv,
                  ffn_w_down, final_gain, batch, seq, tm):
    (kv,) = norm_proj(mem, mem_norm, [xa_wkv.astype(BF16)], [BF16], tm=min(tm, mem.shape[0]))
    x = xattn(x, kv, xa_norm, xa_wq.astype(BF16), xa_wo.astype(BF16), batch, seq, 2 * tm)
    return conv_ffn(x, ffn_norm, ffn_w_up.astype(BF16), ffn_conv, ffn_w_down.astype(BF16), final_gain,
                    batch, seq, tm)


def kernel(x, mem, l0_mix_norm, l0_w_in, l0_ret_norm, l0_s5_lambda_re, l0_s5_lambda_im, l0_s5_b_re, l0_s5_b_im, l0_s5_c_re, l0_s5_c_im, l0_s5_d, l0_s5_log_dt, l0_s5_w_glu, l0_s5_b_glu, l0_w_out, l0_xa_norm, l0_mem_norm, l0_xa_wq, l0_xa_wkv, l0_xa_wo, l0_ffn_norm, l0_ffn_w_up, l0_ffn_conv, l0_ffn_w_down, l1_mix_norm, l1_w_in, l1_conv, l1_a_log, l1_dt_bias, l1_o_norm, l1_w_out, l1_xa_norm, l1_mem_norm, l1_xa_wq, l1_xa_wkv, l1_xa_wo, l1_ffn_norm, l1_ffn_w_up, l1_ffn_conv, l1_ffn_w_down, final_norm):
    batch, seq, d = x.shape
    tm = min(512, seq)
    xf = x.reshape(batch * seq, d)
    memf = mem.reshape(-1, d)
    ret_width = RET_HEADS * RET_HEAD_DIM

    proj, u3 = even_proj(xf, l0_mix_norm, l0_w_in[:, :4 * ret_width].astype(BF16),
                         l0_w_in[:, 4 * ret_width:].T.astype(BF16), tm)
    o_ret = retention(proj, l0_ret_norm, batch, seq, tm)
    y3 = s5_ssm(u3, l0_s5_lambda_re, l0_s5_lambda_im, l0_s5_b_re, l0_s5_b_im, l0_s5_c_re, l0_s5_c_im,
                l0_s5_d, l0_s5_log_dt, batch)
    xf = mix0_out(o_ret, y3, xf, l0_s5_w_glu.astype(BF16), l0_s5_b_glu, l0_w_out.astype(BF16), tm)
    xf = _common_block(xf, memf, l0_xa_norm, l0_mem_norm, l0_xa_wq, l0_xa_wkv, l0_xa_wo, l0_ffn_norm,
                       l0_ffn_w_up, l0_ffn_conv, l0_ffn_w_down, None, batch, seq, tm)

    gdn_width = GDN_HEADS * GDN_HEAD_DIM
    w_qkv = l1_w_in[:, :3 * gdn_width].astype(BF16)
    w_z = l1_w_in[:, 3 * gdn_width:4 * gdn_width].astype(BF16)
    w_b = jnp.pad(l1_w_in[:, 4 * gdn_width:4 * gdn_width + GDN_HEADS], ((0, 0), (0, LANES - GDN_HEADS)))
    w_a = jnp.pad(l1_w_in[:, 4 * gdn_width + GDN_HEADS:], ((0, 0), (0, LANES - GDN_HEADS)))
    w_gates = jnp.concatenate([w_b, w_a], axis=1).astype(BF16)
    qkv, z, gates = gdn_proj(xf, l1_mix_norm, w_qkv, w_z, w_gates, l1_conv, batch, seq, tm)
    xf = gated_deltanet(qkv, z, gates, xf, l1_a_log, l1_dt_bias, l1_o_norm, l1_w_out.astype(BF16),
                        batch, seq, tm)
    xf = _common_block(xf, memf, l1_xa_norm, l1_mem_norm, l1_xa_wq, l1_xa_wkv, l1_xa_wo, l1_ffn_norm,
                       l1_ffn_w_up, l1_ffn_conv, l1_ffn_w_down, final_norm, batch, seq, tm)
    return xf.reshape(batch, seq, d)
```

```python
import functools
import math

import jax
import jax.numpy as jnp
from jax import lax
from jax.experimental import pallas as pl
from jax.experimental.pallas import tpu as pltpu

F32 = jnp.float32
BF16 = jnp.bfloat16
EPS = 1e-6

V7X_VMEM_BYTES = 64 * 1024 * 1024
VMEM_LIMIT_BYTES = V7X_VMEM_BYTES - 8 * 1024 * 1024
LANES = 128
SUBLANES = 8

RET_HEADS = 4
RET_HEAD_DIM = 128
RET_CHUNK = 128
ROPE_BASE = 10000.0
S5_GROUP = 16
S5_STATE = 64
S5_CHUNK = 64
GDN_HEADS = 8
GDN_HEAD_DIM = 128
GDN_CONV = 4
GDN_CHUNK = 64
XA_HEADS = 4
FFN_CONV = 3
FFN_COLS = 256

HIGHEST = lax.Precision.HIGHEST


def _params(*semantics):
    return pltpu.CompilerParams(dimension_semantics=semantics, vmem_limit_bytes=VMEM_LIMIT_BYTES)


def _resident(shape):
    nd = len(shape)
    return pl.BlockSpec(shape, lambda *_: (0,) * nd)


def _dot(a, b, precision=None):
    return jnp.dot(a, b, preferred_element_type=F32, precision=precision)


def _dot_nt(a, b):
    return lax.dot_general(a, b, (((1,), (1,)), ((), ())), preferred_element_type=F32)


def _dot_tn(a, b):
    return lax.dot_general(a, b, (((0,), (0,)), ((), ())), preferred_element_type=F32)


def _rms(x, g):
    return x * lax.rsqrt(jnp.mean(x * x, axis=-1, keepdims=True) + EPS) * g


def _silu(x):
    return x * jax.nn.sigmoid(x)


def _norm_proj_kernel(x_ref, g_ref, *refs):
    n_out = len(refs) // 2
    xn = _rms(x_ref[...], g_ref[...]).astype(BF16)
    for w_ref, o_ref in zip(refs[:n_out], refs[n_out:]):
        o_ref[...] = _dot(xn, w_ref[...]).astype(o_ref.dtype)


def norm_proj(x, gain, weights, out_dtypes, tm):
    t, d = x.shape
    in_specs = [pl.BlockSpec((tm, d), lambda i: (i, 0)), _resident((1, d))]
    in_specs += [_resident(w.shape) for w in weights]
    out_specs = [pl.BlockSpec((tm, w.shape[1]), lambda i: (i, 0)) for w in weights]
    out_shape = [jax.ShapeDtypeStruct((t, w.shape[1]), dt) for w, dt in zip(weights, out_dtypes)]
    return pl.pallas_call(
        _norm_proj_kernel, grid=(t // tm,), in_specs=in_specs, out_specs=out_specs,
        out_shape=out_shape, compiler_params=_params("parallel"), name="norm_proj",
    )(x, gain.reshape(1, d), *weights)


def _even_mixer_kernel(x_ref, g_ref, w_ref, wut_ref, cos_ref, sin_ref, intra_ref, qdec_ref, kdec_ref,
                       cdec_ref, rn_ref, o_ref, u3_ref, proj_ref, state_ref):
    c = RET_CHUNK
    width = RET_HEADS * RET_HEAD_DIM

    @pl.when(pl.program_id(1) == 0)
    def _():
        state_ref[...] = jnp.zeros_like(state_ref)

    xn = _rms(x_ref[...], g_ref[...]).astype(BF16)
    ut = _dot_nt(wut_ref[...], xn)
    for k in range(u3_ref.shape[0]):
        u3_ref[k] = ut[:, k * LANES:(k + 1) * LANES]
    proj_ref[...] = _dot(xn, w_ref[...]).astype(proj_ref.dtype)

    scale = RET_HEAD_DIM ** -0.5
    for ci in range(x_ref.shape[0] // c):
        rows = pl.ds(ci * c, c)
        cos = cos_ref[rows, :]
        sin = sin_ref[rows, :]
        for h in range(RET_HEADS):
            cols = pl.ds(h * RET_HEAD_DIM, RET_HEAD_DIM)

            def part(j):
                return proj_ref[rows, pl.ds(j * width + h * RET_HEAD_DIM, RET_HEAD_DIM)]

            qh = part(0).astype(F32)
            kh = part(1).astype(F32)
            vh = part(2)
            qr = qh * cos + pltpu.roll(qh, RET_HEAD_DIM // 2, 1) * sin
            kr = (kh * cos + pltpu.roll(kh, RET_HEAD_DIM // 2, 1) * sin) * scale
            scores = _dot_nt(qr.astype(BF16), kr.astype(BF16)) * intra_ref[h]
            inner = _dot(scores.astype(BF16), vh)
            st = state_ref[h]
            cross = _dot((qr * qdec_ref[:, cols]).astype(BF16), st.astype(BF16))
            kv = _dot_tn((kr * kdec_ref[:, cols]).astype(BF16), vh)
            state_ref[h] = st * cdec_ref[:, cols] + kv
            o = inner + cross
            o = o * lax.rsqrt(jnp.mean(o * o, axis=-1, keepdims=True) + EPS)
            o_ref[rows, cols] = (o * rn_ref[:, cols] * _silu(part(3).astype(F32))).astype(o_ref.dtype)


def even_mixer_in(x, gain, w, w_u_t, ret_norm, batch, seq, ts):
    t, d = x.shape
    nu = w_u_t.shape[0]
    width = RET_HEADS * RET_HEAD_DIM
    c = RET_CHUNK
    half = RET_HEAD_DIM // 2
    ns = seq // ts
    inv = jnp.exp(-math.log(ROPE_BASE) * jnp.arange(half, dtype=F32) / half)
    ang = jnp.arange(seq, dtype=F32)[:, None] * inv[None, :]
    cos = jnp.concatenate([jnp.cos(ang), jnp.cos(ang)], axis=-1)
    sin = jnp.concatenate([-jnp.sin(ang), jnp.sin(ang)], axis=-1)
    log_gamma = jnp.log1p(-jnp.exp2(-5.0 - jnp.arange(RET_HEADS, dtype=F32)))
    idx = jnp.arange(c, dtype=F32)
    diff = idx[:, None] - idx[None, :]
    causal = diff >= 0
    intra = jnp.where(causal, jnp.exp(log_gamma[:, None, None] * jnp.where(causal, diff, 0.0)), 0.0)

    def per_head_cols(tab):
        return jnp.repeat(tab.T, RET_HEAD_DIM, axis=1)

    qdec = per_head_cols(jnp.exp(log_gamma[:, None] * (idx + 1)))
    kdec = per_head_cols(jnp.exp(log_gamma[:, None] * (c - 1 - idx)))
    cdec = per_head_cols(jnp.exp(log_gamma * c)[:, None])

    in_specs = [pl.BlockSpec((ts, d), lambda b, s: (b * ns + s, 0)), _resident((1, d)), _resident(w.shape),
                _resident(w_u_t.shape),
                pl.BlockSpec((ts, RET_HEAD_DIM), lambda b, s: (s, 0)),
                pl.BlockSpec((ts, RET_HEAD_DIM), lambda b, s: (s, 0)),
                _resident(intra.shape), _resident(qdec.shape), _resident(kdec.shape),
                _resident(cdec.shape), _resident((1, width))]
    return pl.pallas_call(
        _even_mixer_kernel, grid=(batch, ns), in_specs=in_specs,
        out_specs=[pl.BlockSpec((ts, width), lambda b, s: (b * ns + s, 0)),
                   pl.BlockSpec((ts // LANES, nu, LANES), lambda b, s: (b * ns + s, 0, 0))],
        out_shape=[jax.ShapeDtypeStruct((t, width), BF16), jax.ShapeDtypeStruct((t // LANES, nu, LANES), F32)],
        scratch_shapes=[pltpu.VMEM((ts, w.shape[1]), BF16),
                        pltpu.VMEM((RET_HEADS, RET_HEAD_DIM, RET_HEAD_DIM), F32)],
        compiler_params=_params("parallel", "arbitrary"), name="even_mixer_in",
    )(x, gain.reshape(1, d), w, w_u_t, cos, sin, intra, qdec, kdec, cdec, ret_norm.reshape(1, width))


def _s5_prep_kernel(lr_ref, li_ref, ldt_ref, lrc_ref, lic_ref, ldtc_ref, br_ref, bi_ref, cr_ref, ci_ref,
                    ktoe_ref, bdr_ref, bdi_ref, cdr_ref, cdi_ref, al_ref, *, chunk):
    hg = S5_GROUP
    n = chunk * hg
    lr, li = lr_ref[...], li_ref[...]
    dt = jnp.exp(ldt_ref[...])
    mag = jnp.exp(lr * dt)
    a_re = mag * jnp.cos(li * dt)
    a_im = mag * jnp.sin(li * dt)
    den = lr * lr + li * li
    z_re = ((a_re - 1.0) * lr + a_im * li) / den
    z_im = (a_im * lr - (a_re - 1.0) * li) / den

    def zoh(br, bi):
        return z_re * br - z_im * bi, z_re * bi + z_im * br

    e = (chunk - 1.0) - lax.broadcasted_iota(jnp.int32, (chunk, S5_STATE), 0).astype(F32)
    pm = jnp.exp(lr * dt * e)
    pr = jnp.concatenate([pm * jnp.cos(li * dt * e)] * hg, axis=0)
    pi = jnp.concatenate([pm * jnp.sin(li * dt * e)] * hg, axis=0)
    b16_re, b16_im = zoh(br_ref[...], bi_ref[...])

    def over_positions(b):
        return jnp.broadcast_to(b[:, None, :], (hg, chunk, S5_STATE)).reshape(n, S5_STATE)

    bb_re, bb_im = over_positions(b16_re), over_positions(b16_im)
    bdr_ref[...] = (bb_re * pr - bb_im * pi).astype(bdr_ref.dtype)
    bdi_ref[...] = (bb_re * pi + bb_im * pr).astype(bdi_ref.dtype)
    lm = jnp.exp(lr * dt * chunk)
    al_ref[0:1, :] = lm * jnp.cos(li * dt * chunk)
    al_ref[1:2, :] = lm * jnp.sin(li * dt * chunk)

    lrc, lic = lrc_ref[...], lic_ref[...]
    dtc = jnp.exp(ldtc_ref[...])
    tau = lax.broadcasted_iota(jnp.int32, (S5_STATE, chunk), 1).astype(F32)
    vm = jnp.exp(lrc * dtc * tau)
    vc = jnp.concatenate([vm * jnp.cos(lic * dtc * tau)] * hg, axis=1)
    vs = jnp.concatenate([vm * jnp.sin(lic * dtc * tau)] * hg, axis=1)

    def over_lanes(cm):
        return jnp.concatenate([jnp.broadcast_to(cm[:, ho:ho + 1], (S5_STATE, chunk)) for ho in range(hg)], axis=1)

    crr, cri = over_lanes(cr_ref[...]), over_lanes(ci_ref[...])
    v_re = vc * crr - vs * cri
    v_im = vc * cri + vs * crr
    magc = jnp.exp(lrc * dtc)
    ac_re = magc * jnp.cos(lic * dtc)
    ac_im = magc * jnp.sin(lic * dtc)
    cdr_ref[...] = (v_re * ac_re - v_im * ac_im).astype(cdr_ref.dtype)
    cdi_ref[...] = (-(v_re * ac_im + v_im * ac_re)).astype(cdi_ref.dtype)

    p = _dot(b16_re, v_re, HIGHEST) - _dot(b16_im, v_im, HIGHEST)
    row = lax.broadcasted_iota(jnp.int32, (chunk, n), 0)
    keep = lax.broadcasted_iota(jnp.int32, (chunk, n), 1) % chunk >= row
    for hi in range(hg):
        blk = pltpu.roll(jnp.broadcast_to(p[hi:hi + 1, :], (chunk, n)), 0, 1, stride=1, stride_axis=0)
        ktoe_ref[hi * chunk:(hi + 1) * chunk, :] = jnp.where(keep, blk, 0.0).astype(ktoe_ref.dtype)


def _s5_main_kernel(u_ref, ktoe_ref, bdr_ref, bdi_ref, cdr_ref, cdi_ref, al_ref, d_ref, y_ref,
                    sre_ref, sim_ref, *, batch):
    r2 = u_ref.shape[0]
    per_seq = r2 // batch
    c = S5_CHUNK
    pieces = [u_ref[:, hi, :] for hi in range(S5_GROUP)]
    lhs = jnp.concatenate([jnp.concatenate([p[:, 0:c] for p in pieces], axis=1),
                           jnp.concatenate([p[:, c:] for p in pieces], axis=1)], axis=0).astype(BF16)
    sre_ref[...] = _dot(lhs, bdr_ref[...])
    sim_ref[...] = _dot(lhs, bdi_ref[...])
    a_re = al_ref[0:1, :]
    a_im = al_ref[1:2, :]
    s_re = jnp.zeros((batch, S5_STATE), F32)
    s_im = jnp.zeros((batch, S5_STATE), F32)
    for i in range(per_seq):
        for base in (0, r2):
            rows = pl.ds(base + i, batch, stride=per_seq)
            e_re, e_im = sre_ref[rows, :], sim_ref[rows, :]
            sre_ref[rows, :] = s_re
            sim_ref[rows, :] = s_im
            s_re, s_im = a_re * s_re - a_im * s_im + e_re, a_re * s_im + a_im * s_re + e_im
    y = _dot(lhs, ktoe_ref[...])
    y += _dot(sre_ref[...].astype(BF16), cdr_ref[...])
    y += _dot(sim_ref[...].astype(BF16), cdi_ref[...])
    for ho in range(S5_GROUP):
        cols = slice(ho * c, (ho + 1) * c)
        y_ref[:, ho, :] = (jnp.concatenate([y[0:r2, cols], y[r2:, cols]], axis=1)
                           + d_ref[ho:ho + 1, :] * pieces[ho])


def s5_ssm(u3, lam_re, lam_im, b_re, b_im, c_re, c_im, d, log_dt, batch):
    r2, width, lanes = u3.shape
    g, hg, p, chunk = width // S5_GROUP, S5_GROUP, S5_STATE, S5_CHUNK
    assert lanes == LANES == 2 * chunk and r2 % batch == 0
    n = chunk * hg

    def grp(shape):
        return pl.BlockSpec((None,) + shape, lambda i: (i, 0, 0))

    ldt = jnp.broadcast_to(log_dt[:, None, None], (g, 1, p))
    prep_in = [lam_re[:, None, :], lam_im[:, None, :], ldt,
               lam_re[:, :, None], lam_im[:, :, None], jnp.swapaxes(ldt, 1, 2),
               b_re, b_im, c_re, c_im]
    prep_specs = [grp((1, p))] * 3 + [grp((p, 1))] * 3 + [grp((hg, p))] * 2 + [grp((p, hg))] * 2
    ktoe, bdr, bdi, cdr, cdi, al = pl.pallas_call(
        functools.partial(_s5_prep_kernel, chunk=chunk),
        grid=(g,), in_specs=prep_specs,
        out_specs=[grp((n, n)), grp((n, p)), grp((n, p)), grp((p, n)), grp((p, n)), grp((2, p))],
        out_shape=[jax.ShapeDtypeStruct((g, n, n), BF16),
                   jax.ShapeDtypeStruct((g, n, p), BF16), jax.ShapeDtypeStruct((g, n, p), BF16),
                   jax.ShapeDtypeStruct((g, p, n), BF16), jax.ShapeDtypeStruct((g, p, n), BF16),
                   jax.ShapeDtypeStruct((g, 2, p), F32)],
        compiler_params=_params("parallel"), name="s5_prep",
    )(*prep_in)

    channels = pl.BlockSpec((r2, hg, LANES), lambda i: (0, i, 0))
    dvec = jnp.broadcast_to(d[:, :, None], (g, hg, LANES))
    return pl.pallas_call(
        functools.partial(_s5_main_kernel, batch=batch),
        grid=(g,),
        in_specs=[channels, grp((n, n)), grp((n, p)), grp((n, p)), grp((p, n)), grp((p, n)),
                  grp((2, p)), grp((hg, LANES))],
        out_specs=channels,
        out_shape=jax.ShapeDtypeStruct(u3.shape, F32),
        scratch_shapes=[pltpu.VMEM((2 * r2, p), F32), pltpu.VMEM((2 * r2, p), F32)],
        compiler_params=_params("parallel"), name="s5_main",
    )(u3, ktoe, bdr, bdi, cdr, cdi, al, dvec)


def _mix0_rows(o_ref, y3_ref, x, wglu_ref, bglu_ref, wout_ref):
    y = jnp.concatenate([y3_ref[k].T for k in range(y3_ref.shape[0])], axis=0)
    y = 0.5 * y * (1.0 + jnp.tanh(math.sqrt(2.0 / math.pi) * (y + 0.044715 * (y * y * y))))
    y = y * jax.nn.sigmoid(_dot(y.astype(BF16), wglu_ref[...]) + bglu_ref[...])
    merged = jnp.concatenate([o_ref[...], y.astype(BF16)], axis=-1)
    return x + _dot(merged, wout_ref[...])


def _xattn_rows(x, g_ref, wq_ref, kv_ref, wo_ref):
    d = x.shape[1]
    dh = d // XA_HEADS
    xn = _rms(x, g_ref[...]).astype(BF16)
    q = _dot(xn, wq_ref[...]).astype(BF16)
    heads = []
    for h in range(XA_HEADS):
        kh = kv_ref[:, pl.ds(h * dh, dh)]
        vh = kv_ref[:, pl.ds(d + h * dh, dh)]
        s = _dot_nt(q[:, h * dh:(h + 1) * dh], kh) * (dh ** -0.5)
        e = jnp.exp(s - jnp.max(s, axis=-1, keepdims=True))
        p = e * (1.0 / jnp.sum(e, axis=-1, keepdims=True))
        heads.append(_dot(p.astype(BF16), vh).astype(BF16))
    return x + _dot(jnp.concatenate(heads, axis=-1), wo_ref[...])


def _xattn_kernel(x_ref, g_ref, wq_ref, kv_ref, wo_ref, o_ref):
    o_ref[...] = _xattn_rows(x_ref[...], g_ref, wq_ref, kv_ref, wo_ref)


def _mix0_xattn_kernel(o_ref, y3_ref, x_ref, wglu_ref, bglu_ref, wout_ref, g_ref, wq_ref, kv_ref, wo_ref,
                       out_ref):
    x = _mix0_rows(o_ref, y3_ref, x_ref[...], wglu_ref, bglu_ref, wout_ref)
    out_ref[...] = _xattn_rows(x, g_ref, wq_ref, kv_ref, wo_ref)


def xattn(x, kv, gain, wq, wo, batch, seq, tm, mix0=None):
    t, d = x.shape
    n_mem = kv.shape[0] // batch
    ns = seq // tm
    row = pl.BlockSpec((tm, d), lambda b, s: (b * ns + s, 0))
    attn_specs = [_resident((1, d)), _resident(wq.shape), pl.BlockSpec((n_mem, 2 * d), lambda b, s: (b, 0)),
                  _resident(wo.shape)]
    attn_args = (gain.reshape(1, d), wq, kv, wo)
    if mix0 is None:
        body, in_specs, args = _xattn_kernel, [row] + attn_specs, (x,) + attn_args
    else:
        o, y3, w_glu, b_glu, w_out = mix0
        half = o.shape[1]
        body = _mix0_xattn_kernel
        in_specs = [pl.BlockSpec((tm, half), lambda b, s: (b * ns + s, 0)),
                    pl.BlockSpec((tm // LANES, half, LANES), lambda b, s: (b * ns + s, 0, 0)), row,
                    _resident(w_glu.shape), _resident((1, half)), _resident(w_out.shape)] + attn_specs
        args = (o, y3, x, w_glu, b_glu.reshape(1, half), w_out) + attn_args
    return pl.pallas_call(
        body, grid=(batch, ns), in_specs=in_specs, out_specs=row,
        out_shape=jax.ShapeDtypeStruct((t, d), F32),
        compiler_params=_params("parallel", "parallel"), name="xattn",
    )(*args)


def _ffn_kernel(x_ref, g_ref, wup_ref, cw_ref, wd_ref, fg_ref, o_ref, xn_ref, tail_ref, act_ref,
                *, final_norm):
    tm = x_ref.shape[0]
    f = wd_ref.shape[0]
    fc = FFN_COLS
    pad = SUBLANES

    @pl.when(pl.program_id(1) == 0)
    def _():
        tail_ref[...] = jnp.zeros_like(tail_ref)

    xn_ref[...] = _rms(x_ref[...], g_ref[...]).astype(BF16)

    def proj_conv(col0):
        cols = pl.ds(col0, fc)
        cur = _dot(xn_ref[...], wup_ref[:, cols])
        z = jnp.concatenate([tail_ref[:, cols], cur], axis=0)
        tail_ref[:, cols] = cur[tm - pad:, :]
        y = cw_ref[0:1, cols] * z
        y = cw_ref[1:2, cols] * z + pltpu.roll(y, 1, 0)
        y = cw_ref[2:3, cols] * z + pltpu.roll(y, 1, 0)
        return y[pad:, :]

    for cb in range(f // fc):
        up = proj_conv(cb * fc)
        gate = proj_conv(f + cb * fc)
        act_ref[:, pl.ds(cb * fc, fc)] = (_silu(gate) * up).astype(BF16)
    y = x_ref[...] + _dot(act_ref[...], wd_ref[...])
    if final_norm:
        y = _rms(y, fg_ref[...])
    o_ref[...] = y


def conv_ffn(x, gain, w_up, conv_w, w_down, final_gain, batch, seq, tm):
    t, d = x.shape
    f = w_down.shape[0]
    assert FFN_CONV == conv_w.shape[0] == 3 and f % FFN_COLS == 0
    ns = seq // tm
    final_norm = final_gain is not None
    fg = (final_gain if final_norm else jnp.ones((d,), F32)).reshape(1, d)
    row = pl.BlockSpec((tm, d), lambda b, s: (b * ns + s, 0))
    return pl.pallas_call(
        functools.partial(_ffn_kernel, final_norm=final_norm), grid=(batch, ns),
        in_specs=[row, _resident((1, d)), _resident(w_up.shape), _resident(conv_w.shape),
                  _resident(w_down.shape), _resident((1, d))],
        out_specs=row, out_shape=jax.ShapeDtypeStruct((t, d), F32),
        scratch_shapes=[pltpu.VMEM((tm, d), BF16), pltpu.VMEM((SUBLANES, 2 * f), F32),
                        pltpu.VMEM((tm, f), BF16)],
        compiler_params=_params("parallel", "arbitrary"), name="conv_ffn",
    )(x, gain.reshape(1, d), w_up, conv_w, w_down, fg)


def _unit_lower_inverses(a_list, eye):
    n = eye.shape[0]
    ms = [eye - a for a in a_list]
    ps = [(-a).astype(BF16) for a in a_list]
    ps = [_dot(p, p).astype(BF16) for p in ps]
    for _ in range(4):
        both = [_dot(jnp.concatenate([m.astype(BF16), p], axis=0), p) for m, p in zip(ms, ps)]
        ms = [m + b[0:n] for m, b in zip(ms, both)]
        ps = [b[n:].astype(BF16) for b in both]
    return [m + _dot(m.astype(BF16), p) for m, p in zip(ms, ps)]


def _gdn_proj_kernel(x_ref, g_ref, wqkv_ref, wz_ref, wg_ref, cw_ref, qkv_ref, z_ref, gates_ref,
                     xn_ref, tail_ref):
    tm = x_ref.shape[0]
    width = GDN_HEADS * GDN_HEAD_DIM
    dh = GDN_HEAD_DIM
    cb_cols = 2 * dh
    pad = SUBLANES

    @pl.when(pl.program_id(1) == 0)
    def _():
        tail_ref[...] = jnp.zeros_like(tail_ref)

    xn_ref[...] = _rms(x_ref[...], g_ref[...]).astype(BF16)
    for cb in range(3 * width // cb_cols):
        cols = pl.ds(cb * cb_cols, cb_cols)
        cur = _dot(xn_ref[...], wqkv_ref[:, cols])
        zz = jnp.concatenate([tail_ref[:, cols], cur], axis=0)
        tail_ref[:, cols] = cur[tm - pad:, :]
        zs = pltpu.roll(zz, 1, 0)
        near = cw_ref[3:4, cols] * zz + cw_ref[2:3, cols] * zs
        far = cw_ref[1:2, cols] * zz + cw_ref[0:1, cols] * zs
        y = _silu((near + pltpu.roll(far, 2, 0))[pad:, :])
        if cb * cb_cols < 2 * width:
            scale = dh ** -0.5 if cb * cb_cols < width else 1.0
            parts = []
            for j in range(cb_cols // dh):
                yh = y[:, j * dh:(j + 1) * dh]
                parts.append(yh * (lax.rsqrt(jnp.sum(yh * yh, axis=-1, keepdims=True) + EPS) * scale))
            y = jnp.concatenate(parts, axis=1)
        qkv_ref[:, cols] = y.astype(qkv_ref.dtype)
        if cb % 3 == 2:
            zc = pl.ds((cb // 3) * cb_cols, cb_cols)
            z_ref[:, zc] = _dot(xn_ref[...], wz_ref[:, zc]).astype(z_ref.dtype)
        if cb == 0:
            gates_ref[...] = _dot(xn_ref[...], wg_ref[...])


def gdn_proj(x, gain, w_qkv, w_z, w_gates, conv_w, batch, seq, tm):
    t, d = x.shape
    ns = seq // tm
    assert conv_w.shape[0] == GDN_CONV

    def row(w):
        return pl.BlockSpec((tm, w), lambda b, s: (b * ns + s, 0))

    widths = [w_qkv.shape[1], w_z.shape[1], w_gates.shape[1]]
    return pl.pallas_call(
        _gdn_proj_kernel, grid=(batch, ns),
        in_specs=[row(d), _resident((1, d)), _resident(w_qkv.shape), _resident(w_z.shape),
                  _resident(w_gates.shape), _resident(conv_w.shape)],
        out_specs=[row(w) for w in widths],
        out_shape=[jax.ShapeDtypeStruct((t, w), dt) for w, dt in zip(widths, (BF16, BF16, F32))],
        scratch_shapes=[pltpu.VMEM((tm, d), BF16), pltpu.VMEM((SUBLANES, w_qkv.shape[1]), F32)],
        compiler_params=_params("parallel", "arbitrary"), name="gdn_proj",
    )(x, gain.reshape(1, d), w_qkv, w_z, w_gates, conv_w)


def _gdn_kernel(qkv_ref, z_ref, b_ref, a_ref, x_ref, alog_ref, dtb_ref, on_ref, wout_ref, o_ref,
                q_ref, k_ref, v_ref, w_ref, qk_ref, gc_ref, gct_ref, beta_ref, ob_ref, state_ref):
    ts = qkv_ref.shape[0]
    width = GDN_HEADS * GDN_HEAD_DIM
    dh = GDN_HEAD_DIM
    c = GDN_CHUNK
    heads = range(GDN_HEADS)

    @pl.when(pl.program_id(1) == 0)
    def _():
        state_ref[...] = jnp.zeros_like(state_ref)

    beta_ref[...] = jax.nn.sigmoid(b_ref[...])
    ag = a_ref[...] + dtb_ref[...]
    softplus = jnp.maximum(ag, 0.0) + jnp.log1p(jnp.exp(-jnp.abs(ag)))
    gc = -(jnp.exp(alog_ref[...]) * softplus)
    pos = lax.broadcasted_iota(jnp.int32, (ts, LANES), 0) % c
    shift = 1
    while shift < c:
        gc = gc + jnp.where(pos >= shift, pltpu.roll(gc, shift, 0), 0.0)
        shift *= 2
    gc_ref[...] = gc
    gct_ref[...] = gc.T

    ri = lax.broadcasted_iota(jnp.int32, (2 * c, 2 * c), 0)
    ci = lax.broadcasted_iota(jnp.int32, (2 * c, 2 * c), 1)
    same = (ri // c) == (ci // c)
    incl = same & (ri >= ci)
    strict = same & (ri > ci)
    eye = jnp.where(ri == ci, 1.0, 0.0).astype(F32)

    def hcols(h):
        return pl.ds(h * dh, dh)

    def wy_factors(dc):
        r0 = dc * 2 * c
        rows = pl.ds(r0, 2 * c)
        gcd = gc_ref[rows, :]
        gctd = gct_ref[:, rows]
        bet = beta_ref[rows, :]
        qbf = [qkv_ref[rows, hcols(h)] for h in heads]
        kbf = [qkv_ref[rows, hcols(GDN_HEADS + h)] for h in heads]
        qs = [q.astype(F32) for q in qbf]
        ks = [k.astype(F32) for k in kbf]
        vs = [qkv_ref[rows, hcols(2 * GDN_HEADS + h)].astype(F32) for h in heads]
        gcols = [gcd[:, h:h + 1] for h in heads]
        bcols = [bet[:, h:h + 1] for h in heads]
        decays = [jnp.where(incl, jnp.exp(jnp.where(incl, gcols[h] - gctd[h:h + 1, :], 0.0)), 0.0)
                  for h in heads]
        kbs = [ks[h] * bcols[h] for h in heads]
        both = [_dot_nt(jnp.concatenate([kbs[h].astype(BF16), qbf[h]], axis=0), kbf[h]) for h in heads]
        a_mats = [jnp.where(strict, both[h][0:2 * c] * decays[h], 0.0) for h in heads]
        for h in heads:
            qk_ref[rows, hcols(h)] = jnp.where(incl, both[h][2 * c:] * decays[h], 0.0).astype(BF16)
        t_mats = [t.astype(BF16) for t in _unit_lower_inverses(a_mats, eye)]
        egs = [jnp.exp(g) for g in gcols]
        for h in heads:
            w_ref[rows, hcols(h)] = _dot(t_mats[h], (kbs[h] * egs[h]).astype(BF16)).astype(BF16)
        for h in heads:
            v_ref[rows, hcols(h)] = _dot(t_mats[h], (vs[h] * bcols[h]).astype(BF16))
        for h in heads:
            q_ref[rows, hcols(h)] = (qs[h] * egs[h]).astype(BF16)
            gl0 = gcd[c - 1:c, h:h + 1]
            gl1 = gcd[2 * c - 1:2 * c, h:h + 1]
            glcol = jnp.concatenate([jnp.broadcast_to(gl0, (c, 1)), jnp.broadcast_to(gl1, (c, 1))], axis=0)
            k_ref[rows, hcols(h)] = (ks[h] * jnp.exp(glcol - gcols[h])).astype(BF16)

    def recurrence(dc):
        r0 = dc * 2 * c
        rows = pl.ds(r0, 2 * c)
        sts = [state_ref[h] for h in heads]
        vns, outs = [], []
        for half in range(2):
            r = pl.ds(r0 + half * c, c)
            egl = jnp.exp(gc_ref[pl.ds(r0 + half * c + c - 1, 1), :])
            sbs = [s.astype(BF16) for s in sts]
            ws = [_dot(jnp.concatenate([w_ref[r, hcols(h)], q_ref[r, hcols(h)]], axis=0), sbs[h]) for h in heads]
            vn = [v_ref[r, hcols(h)] - ws[h][0:c] for h in heads]
            outs.append([ws[h][c:] for h in heads])
            sts = [sts[h] * egl[:, h:h + 1] + _dot_tn(k_ref[r, hcols(h)], vn[h].astype(BF16))
                   for h in heads]
            vns.append(vn)
        for h in heads:
            state_ref[h] = sts[h]
        for h in heads:
            vn = jnp.concatenate([vns[0][h], vns[1][h]], axis=0).astype(BF16)
            o = jnp.concatenate([outs[0][h], outs[1][h]], axis=0) + _dot(qk_ref[rows, hcols(h)], vn)
            o = o * lax.rsqrt(jnp.mean(o * o, axis=-1, keepdims=True) + EPS) * on_ref[...]
            ob_ref[rows, hcols(h)] = (o * _silu(z_ref[rows, hcols(h)].astype(F32))).astype(BF16)

    n_pairs = ts // (2 * c)
    wy_factors(0)
    for dc in range(n_pairs):
        if dc + 1 < n_pairs:
            wy_factors(dc + 1)
        recurrence(dc)
    o_ref[...] = x_ref[...] + _dot(ob_ref[...], wout_ref[...])


def gated_deltanet(qkv, z, gates, x, a_log, dt_bias, o_norm, w_out, batch, seq, ts):
    t, d = x.shape
    width = GDN_HEADS * GDN_HEAD_DIM
    ns = seq // ts

    def lane_pad(vec):
        return jnp.pad(vec, (0, LANES - vec.shape[0])).reshape(1, LANES)

    def row(w, j):
        return pl.BlockSpec((ts, w), lambda b, s: (b * ns + s, j))

    return pl.pallas_call(
        _gdn_kernel, grid=(batch, ns),
        in_specs=[row(3 * width, 0), row(width, 0), row(LANES, 0), row(LANES, 1), row(d, 0),
                  _resident((1, LANES)), _resident((1, LANES)), _resident((1, GDN_HEAD_DIM)),
                  _resident(w_out.shape)],
        out_specs=row(d, 0), out_shape=jax.ShapeDtypeStruct((t, d), F32),
        scratch_shapes=[pltpu.VMEM((ts, width), BF16), pltpu.VMEM((ts, width), BF16),
                        pltpu.VMEM((ts, width), F32),
                        pltpu.VMEM((ts, width), BF16), pltpu.VMEM((ts, width), BF16),
                        pltpu.VMEM((ts, LANES), F32), pltpu.VMEM((LANES, ts), F32),
                        pltpu.VMEM((ts, LANES), F32), pltpu.VMEM((ts, width), BF16),
                        pltpu.VMEM((GDN_HEADS, GDN_HEAD_DIM, GDN_HEAD_DIM), F32)],
        compiler_params=_params("parallel", "arbitrary"), name="gated_deltanet",
    )(qkv, z, gates, gates, x, lane_pad(a_log), lane_pad(dt_bias), o_norm.reshape(1, GDN_HEAD_DIM), w_out)


def _common_block(x, mem, xa_norm, mem_norm, xa_wq, xa_wkv, xa_wo, ffn_norm, ffn_w_up, ffn_conv,
                  ffn_w_down, final_gain, batch, seq, tm, mix0=None):
    (kv,) = norm_proj(mem, mem_norm, [xa_wkv.astype(BF16)], [BF16], tm=min(tm, mem.shape[0]))
    x = xattn(x, kv, xa_norm, xa_wq.astype(BF16), xa_wo.astype(BF16), batch, seq, 2 * tm, mix0)
    return conv_ffn(x, ffn_norm, ffn_w_up.astype(BF16), ffn_conv, ffn_w_down.astype(BF16), final_gain,
                    batch, seq, tm)


def kernel(x, mem, l0_mix_norm, l0_w_in, l0_ret_norm, l0_s5_lambda_re, l0_s5_lambda_im, l0_s5_b_re, l0_s5_b_im, l0_s5_c_re, l0_s5_c_im, l0_s5_d, l0_s5_log_dt, l0_s5_w_glu, l0_s5_b_glu, l0_w_out, l0_xa_norm, l0_mem_norm, l0_xa_wq, l0_xa_wkv, l0_xa_wo, l0_ffn_norm, l0_ffn_w_up, l0_ffn_conv, l0_ffn_w_down, l1_mix_norm, l1_w_in, l1_conv, l1_a_log, l1_dt_bias, l1_o_norm, l1_w_out, l1_xa_norm, l1_mem_norm, l1_xa_wq, l1_xa_wkv, l1_xa_wo, l1_ffn_norm, l1_ffn_w_up, l1_ffn_conv, l1_ffn_w_down, final_norm):
    batch, seq, d = x.shape
    tm = min(512, seq)
    xf = x.reshape(batch * seq, d)
    memf = mem.reshape(-1, d)
    ret_width = RET_HEADS * RET_HEAD_DIM

    o_ret, u3 = even_mixer_in(xf, l0_mix_norm, l0_w_in[:, :4 * ret_width].astype(BF16),
                              l0_w_in[:, 4 * ret_width:].T.astype(BF16), l0_ret_norm, batch, seq, tm)
    y3 = s5_ssm(u3, l0_s5_lambda_re, l0_s5_lambda_im, l0_s5_b_re, l0_s5_b_im, l0_s5_c_re, l0_s5_c_im,
                l0_s5_d, l0_s5_log_dt, batch)
    mix0 = (o_ret, y3, l0_s5_w_glu.astype(BF16), l0_s5_b_glu, l0_w_out.astype(BF16))
    xf = _common_block(xf, memf, l0_xa_norm, l0_mem_norm, l0_xa_wq, l0_xa_wkv, l0_xa_wo, l0_ffn_norm,
                       l0_ffn_w_up, l0_ffn_conv, l0_ffn_w_down, None, batch, seq, tm, mix0)

    gdn_width = GDN_HEADS * GDN_HEAD_DIM
    w_qkv = l1_w_in[:, :3 * gdn_width].astype(BF16)
    w_z = l1_w_in[:, 3 * gdn_width:4 * gdn_width].astype(BF16)
    w_b = jnp.pad(l1_w_in[:, 4 * gdn_width:4 * gdn_width + GDN_HEADS], ((0, 0), (0, LANES - GDN_HEADS)))
    w_a = jnp.pad(l1_w_in[:, 4 * gdn_width + GDN_HEADS:], ((0, 0), (0, LANES - GDN_HEADS)))
    w_gates = jnp.concatenate([w_b, w_a], axis=1).astype(BF16)
    qkv, z, gates = gdn_proj(xf, l1_mix_norm, w_qkv, w_z, w_gates, l1_conv, batch, seq, tm)
    xf = gated_deltanet(qkv, z, gates, xf, l1_a_log, l1_dt_bias, l1_o_norm, l1_w_out.astype(BF16),
                        batch, seq, tm)
    xf = _common_block(xf, memf, l1_xa_norm, l1_mem_norm, l1_xa_wq, l1_xa_wkv, l1_xa_wo, l1_ffn_norm,
                       l1_ffn_w_up, l1_ffn_conv, l1_ffn_w_down, final_norm, batch, seq, tm)
    return xf.reshape(batch, seq, d)
```

```python
import functools
import math

import jax
import jax.numpy as jnp
from jax import lax
from jax.experimental import pallas as pl
from jax.experimental.pallas import tpu as pltpu

F32 = jnp.float32
BF16 = jnp.bfloat16
EPS = 1e-6

V7X_VMEM_BYTES = 64 * 1024 * 1024
VMEM_LIMIT_BYTES = V7X_VMEM_BYTES - 8 * 1024 * 1024
LANES = 128
SUBLANES = 8

RET_HEADS = 4
RET_HEAD_DIM = 128
RET_CHUNK = 128
ROPE_BASE = 10000.0
S5_GROUP = 16
S5_STATE = 64
S5_CHUNK = 64
GDN_HEADS = 8
GDN_HEAD_DIM = 128
GDN_CONV = 4
GDN_CHUNK = 64
XA_HEADS = 4
FFN_CONV = 3
FFN_COLS = 256
ROW_TILE = 1024
GDN_ROW_TILE = 512

HIGHEST = lax.Precision.HIGHEST


def _params(*semantics):
    return pltpu.CompilerParams(dimension_semantics=semantics, vmem_limit_bytes=VMEM_LIMIT_BYTES)


def _resident(shape):
    nd = len(shape)
    return pl.BlockSpec(shape, lambda *_: (0,) * nd)


def _dot(a, b, precision=None):
    return jnp.dot(a, b, preferred_element_type=F32, precision=precision)


def _dot_nt(a, b):
    return lax.dot_general(a, b, (((1,), (1,)), ((), ())), preferred_element_type=F32)


def _dot_tn(a, b):
    return lax.dot_general(a, b, (((0,), (0,)), ((), ())), preferred_element_type=F32)


def _rms(x, g):
    return x * lax.rsqrt(jnp.mean(x * x, axis=-1, keepdims=True) + EPS) * g


def _silu(x):
    h = 0.5 * x
    return h + h * jnp.tanh(h)


def _norm_proj_kernel(x_ref, g_ref, *refs):
    n_out = len(refs) // 2
    xn = _rms(x_ref[...], g_ref[...]).astype(BF16)
    for w_ref, o_ref in zip(refs[:n_out], refs[n_out:]):
        o_ref[...] = _dot(xn, w_ref[...]).astype(o_ref.dtype)


def norm_proj(x, gain, weights, out_dtypes, tm):
    t, d = x.shape
    in_specs = [pl.BlockSpec((tm, d), lambda i: (i, 0)), _resident((1, d))]
    in_specs += [_resident(w.shape) for w in weights]
    out_specs = [pl.BlockSpec((tm, w.shape[1]), lambda i: (i, 0)) for w in weights]
    out_shape = [jax.ShapeDtypeStruct((t, w.shape[1]), dt) for w, dt in zip(weights, out_dtypes)]
    return pl.pallas_call(
        _norm_proj_kernel, grid=(t // tm,), in_specs=in_specs, out_specs=out_specs,
        out_shape=out_shape, compiler_params=_params("parallel"), name="norm_proj",
    )(x, gain.reshape(1, d), *weights)


def _even_mixer_kernel(x_ref, g_ref, w_ref, wut_ref, cos_ref, sin_ref, intra_ref, qdec_ref, kdec_ref,
                       cdec_ref, rn_ref, o_ref, u3_ref, proj_ref, state_ref):
    c = RET_CHUNK
    width = RET_HEADS * RET_HEAD_DIM

    @pl.when(pl.program_id(1) == 0)
    def _():
        state_ref[...] = jnp.zeros_like(state_ref)

    xn = _rms(x_ref[...], g_ref[...]).astype(BF16)
    ut = _dot_nt(wut_ref[...], xn)
    for k in range(u3_ref.shape[0]):
        u3_ref[k] = ut[:, k * LANES:(k + 1) * LANES]
    proj_ref[...] = _dot(xn, w_ref[...]).astype(proj_ref.dtype)

    scale = RET_HEAD_DIM ** -0.5
    for ci in range(x_ref.shape[0] // c):
        rows = pl.ds(ci * c, c)
        cos = cos_ref[rows, :]
        sin = sin_ref[rows, :]
        for h in range(RET_HEADS):
            cols = pl.ds(h * RET_HEAD_DIM, RET_HEAD_DIM)

            def part(j):
                return proj_ref[rows, pl.ds(j * width + h * RET_HEAD_DIM, RET_HEAD_DIM)]

            qh = part(0).astype(F32)
            kh = part(1).astype(F32)
            vh = part(2)
            qr = qh * cos + pltpu.roll(qh, RET_HEAD_DIM // 2, 1) * sin
            kr = (kh * cos + pltpu.roll(kh, RET_HEAD_DIM // 2, 1) * sin) * scale
            scores = _dot_nt(qr.astype(BF16), kr.astype(BF16)) * intra_ref[h]
            inner = _dot(scores.astype(BF16), vh)
            st = state_ref[h]
            cross = _dot((qr * qdec_ref[:, cols]).astype(BF16), st.astype(BF16))
            kv = _dot_tn((kr * kdec_ref[:, cols]).astype(BF16), vh)
            state_ref[h] = st * cdec_ref[:, cols] + kv
            o = inner + cross
            o = o * lax.rsqrt(jnp.mean(o * o, axis=-1, keepdims=True) + EPS)
            o_ref[rows, cols] = (o * rn_ref[:, cols] * _silu(part(3).astype(F32))).astype(o_ref.dtype)


def even_mixer_in(x, gain, w, w_u_t, ret_norm, batch, seq, ts):
    t, d = x.shape
    nu = w_u_t.shape[0]
    width = RET_HEADS * RET_HEAD_DIM
    c = RET_CHUNK
    half = RET_HEAD_DIM // 2
    ns = seq // ts
    inv = jnp.exp(-math.log(ROPE_BASE) * jnp.arange(half, dtype=F32) / half)
    ang = jnp.arange(seq, dtype=F32)[:, None] * inv[None, :]
    cos = jnp.concatenate([jnp.cos(ang), jnp.cos(ang)], axis=-1)
    sin = jnp.concatenate([-jnp.sin(ang), jnp.sin(ang)], axis=-1)
    log_gamma = jnp.log1p(-jnp.exp2(-5.0 - jnp.arange(RET_HEADS, dtype=F32)))
    idx = jnp.arange(c, dtype=F32)
    diff = idx[:, None] - idx[None, :]
    causal = diff >= 0
    intra = jnp.where(causal, jnp.exp(log_gamma[:, None, None] * jnp.where(causal, diff, 0.0)), 0.0)

    def per_head_cols(tab):
        return jnp.repeat(tab.T, RET_HEAD_DIM, axis=1)

    qdec = per_head_cols(jnp.exp(log_gamma[:, None] * (idx + 1)))
    kdec = per_head_cols(jnp.exp(log_gamma[:, None] * (c - 1 - idx)))
    cdec = per_head_cols(jnp.exp(log_gamma * c)[:, None])

    in_specs = [pl.BlockSpec((ts, d), lambda b, s: (b * ns + s, 0)), _resident((1, d)), _resident(w.shape),
                _resident(w_u_t.shape),
                pl.BlockSpec((ts, RET_HEAD_DIM), lambda b, s: (s, 0)),
                pl.BlockSpec((ts, RET_HEAD_DIM), lambda b, s: (s, 0)),
                _resident(intra.shape), _resident(qdec.shape), _resident(kdec.shape),
                _resident(cdec.shape), _resident((1, width))]
    return pl.pallas_call(
        _even_mixer_kernel, grid=(batch, ns), in_specs=in_specs,
        out_specs=[pl.BlockSpec((ts, width), lambda b, s: (b * ns + s, 0)),
                   pl.BlockSpec((ts // LANES, nu, LANES), lambda b, s: (b * ns + s, 0, 0))],
        out_shape=[jax.ShapeDtypeStruct((t, width), BF16), jax.ShapeDtypeStruct((t // LANES, nu, LANES), F32)],
        scratch_shapes=[pltpu.VMEM((ts, w.shape[1]), BF16),
                        pltpu.VMEM((RET_HEADS, RET_HEAD_DIM, RET_HEAD_DIM), F32)],
        compiler_params=_params("parallel", "arbitrary"), name="even_mixer_in",
    )(x, gain.reshape(1, d), w, w_u_t, cos, sin, intra, qdec, kdec, cdec, ret_norm.reshape(1, width))


def _s5_prep_kernel(lr_ref, li_ref, ldt_ref, lrc_ref, lic_ref, ldtc_ref, br_ref, bi_ref, cr_ref, ci_ref,
                    ktoe_ref, bdr_ref, bdi_ref, cdr_ref, cdi_ref, al_ref, *, chunk):
    hg = S5_GROUP
    n = chunk * hg
    lr, li = lr_ref[...], li_ref[...]
    dt = jnp.exp(ldt_ref[...])
    mag = jnp.exp(lr * dt)
    a_re = mag * jnp.cos(li * dt)
    a_im = mag * jnp.sin(li * dt)
    den = lr * lr + li * li
    z_re = ((a_re - 1.0) * lr + a_im * li) / den
    z_im = (a_im * lr - (a_re - 1.0) * li) / den

    def zoh(br, bi):
        return z_re * br - z_im * bi, z_re * bi + z_im * br

    e = (chunk - 1.0) - lax.broadcasted_iota(jnp.int32, (chunk, S5_STATE), 0).astype(F32)
    pm = jnp.exp(lr * dt * e)
    pr = jnp.concatenate([pm * jnp.cos(li * dt * e)] * hg, axis=0)
    pi = jnp.concatenate([pm * jnp.sin(li * dt * e)] * hg, axis=0)
    b16_re, b16_im = zoh(br_ref[...], bi_ref[...])

    def over_positions(b):
        return jnp.broadcast_to(b[:, None, :], (hg, chunk, S5_STATE)).reshape(n, S5_STATE)

    bb_re, bb_im = over_positions(b16_re), over_positions(b16_im)
    bdr_ref[...] = (bb_re * pr - bb_im * pi).astype(bdr_ref.dtype)
    bdi_ref[...] = (bb_re * pi + bb_im * pr).astype(bdi_ref.dtype)
    lm = jnp.exp(lr * dt * chunk)
    al_ref[0:1, :] = lm * jnp.cos(li * dt * chunk)
    al_ref[1:2, :] = lm * jnp.sin(li * dt * chunk)

    lrc, lic = lrc_ref[...], lic_ref[...]
    dtc = jnp.exp(ldtc_ref[...])
    tau = lax.broadcasted_iota(jnp.int32, (S5_STATE, chunk), 1).astype(F32)
    vm = jnp.exp(lrc * dtc * tau)
    vc = jnp.concatenate([vm * jnp.cos(lic * dtc * tau)] * hg, axis=1)
    vs = jnp.concatenate([vm * jnp.sin(lic * dtc * tau)] * hg, axis=1)

    def over_lanes(cm):
        return jnp.concatenate([jnp.broadcast_to(cm[:, ho:ho + 1], (S5_STATE, chunk)) for ho in range(hg)], axis=1)

    crr, cri = over_lanes(cr_ref[...]), over_lanes(ci_ref[...])
    v_re = vc * crr - vs * cri
    v_im = vc * cri + vs * crr
    magc = jnp.exp(lrc * dtc)
    ac_re = magc * jnp.cos(lic * dtc)
    ac_im = magc * jnp.sin(lic * dtc)
    cdr_ref[...] = (v_re * ac_re - v_im * ac_im).astype(cdr_ref.dtype)
    cdi_ref[...] = (-(v_re * ac_im + v_im * ac_re)).astype(cdi_ref.dtype)

    p = _dot(b16_re, v_re, HIGHEST) - _dot(b16_im, v_im, HIGHEST)
    row = lax.broadcasted_iota(jnp.int32, (chunk, n), 0)
    keep = lax.broadcasted_iota(jnp.int32, (chunk, n), 1) % chunk >= row
    for hi in range(hg):
        blk = pltpu.roll(jnp.broadcast_to(p[hi:hi + 1, :], (chunk, n)), 0, 1, stride=1, stride_axis=0)
        ktoe_ref[hi * chunk:(hi + 1) * chunk, :] = jnp.where(keep, blk, 0.0).astype(ktoe_ref.dtype)


def _s5_main_kernel(u_ref, ktoe_ref, bdr_ref, bdi_ref, cdr_ref, cdi_ref, al_ref, d_ref, y_ref,
                    sre_ref, sim_ref, *, batch):
    r2 = u_ref.shape[0]
    per_seq = r2 // batch
    c = S5_CHUNK
    pieces = [u_ref[:, hi, :] for hi in range(S5_GROUP)]
    lhs = jnp.concatenate([jnp.concatenate([p[:, 0:c] for p in pieces], axis=1),
                           jnp.concatenate([p[:, c:] for p in pieces], axis=1)], axis=0).astype(BF16)
    sre_ref[...] = _dot(lhs, bdr_ref[...])
    sim_ref[...] = _dot(lhs, bdi_ref[...])
    a_re = al_ref[0:1, :]
    a_im = al_ref[1:2, :]
    s_re = jnp.zeros((batch, S5_STATE), F32)
    s_im = jnp.zeros((batch, S5_STATE), F32)
    for i in range(per_seq):
        for base in (0, r2):
            rows = pl.ds(base + i, batch, stride=per_seq)
            e_re, e_im = sre_ref[rows, :], sim_ref[rows, :]
            sre_ref[rows, :] = s_re
            sim_ref[rows, :] = s_im
            s_re, s_im = a_re * s_re - a_im * s_im + e_re, a_re * s_im + a_im * s_re + e_im
    y = _dot(lhs, ktoe_ref[...])
    y += _dot(sre_ref[...].astype(BF16), cdr_ref[...])
    y += _dot(sim_ref[...].astype(BF16), cdi_ref[...])
    for ho in range(S5_GROUP):
        cols = slice(ho * c, (ho + 1) * c)
        y_ref[:, ho, :] = (jnp.concatenate([y[0:r2, cols], y[r2:, cols]], axis=1)
                           + d_ref[ho:ho + 1, :] * pieces[ho])


def s5_ssm(u3, lam_re, lam_im, b_re, b_im, c_re, c_im, d, log_dt, batch):
    r2, width, lanes = u3.shape
    g, hg, p, chunk = width // S5_GROUP, S5_GROUP, S5_STATE, S5_CHUNK
    assert lanes == LANES == 2 * chunk and r2 % batch == 0
    n = chunk * hg

    def grp(shape):
        return pl.BlockSpec((None,) + shape, lambda i: (i, 0, 0))

    ldt = jnp.broadcast_to(log_dt[:, None, None], (g, 1, p))
    prep_in = [lam_re[:, None, :], lam_im[:, None, :], ldt,
               lam_re[:, :, None], lam_im[:, :, None], jnp.swapaxes(ldt, 1, 2),
               b_re, b_im, c_re, c_im]
    prep_specs = [grp((1, p))] * 3 + [grp((p, 1))] * 3 + [grp((hg, p))] * 2 + [grp((p, hg))] * 2
    ktoe, bdr, bdi, cdr, cdi, al = pl.pallas_call(
        functools.partial(_s5_prep_kernel, chunk=chunk),
        grid=(g,), in_specs=prep_specs,
        out_specs=[grp((n, n)), grp((n, p)), grp((n, p)), grp((p, n)), grp((p, n)), grp((2, p))],
        out_shape=[jax.ShapeDtypeStruct((g, n, n), BF16),
                   jax.ShapeDtypeStruct((g, n, p), BF16), jax.ShapeDtypeStruct((g, n, p), BF16),
                   jax.ShapeDtypeStruct((g, p, n), BF16), jax.ShapeDtypeStruct((g, p, n), BF16),
                   jax.ShapeDtypeStruct((g, 2, p), F32)],
        compiler_params=_params("parallel"), name="s5_prep",
    )(*prep_in)

    channels = pl.BlockSpec((r2, hg, LANES), lambda i: (0, i, 0))
    dvec = jnp.broadcast_to(d[:, :, None], (g, hg, LANES))
    return pl.pallas_call(
        functools.partial(_s5_main_kernel, batch=batch),
        grid=(g,),
        in_specs=[channels, grp((n, n)), grp((n, p)), grp((n, p)), grp((p, n)), grp((p, n)),
                  grp((2, p)), grp((hg, LANES))],
        out_specs=channels,
        out_shape=jax.ShapeDtypeStruct(u3.shape, F32),
        scratch_shapes=[pltpu.VMEM((2 * r2, p), F32), pltpu.VMEM((2 * r2, p), F32)],
        compiler_params=_params("parallel"), name="s5_main",
    )(u3, ktoe, bdr, bdi, cdr, cdi, al, dvec)


def _mix0_rows(o_ref, y3_ref, x, wglu_ref, bglu_ref, wout_ref):
    y = jnp.concatenate([y3_ref[k].T for k in range(y3_ref.shape[0])], axis=0)
    y = 0.5 * y * (1.0 + jnp.tanh(math.sqrt(2.0 / math.pi) * (y + 0.044715 * (y * y * y))))
    y = y * jax.nn.sigmoid(_dot(y.astype(BF16), wglu_ref[...]) + bglu_ref[...])
    merged = jnp.concatenate([o_ref[...], y.astype(BF16)], axis=-1)
    return x + _dot(merged, wout_ref[...])


def _xattn_rows(x, g_ref, wq_ref, kv_ref, wo_ref):
    d = x.shape[1]
    dh = d // XA_HEADS
    xn = _rms(x, g_ref[...]).astype(BF16)
    q = _dot(xn, wq_ref[...]).astype(BF16)
    heads = []
    for h in range(XA_HEADS):
        kh = kv_ref[:, pl.ds(h * dh, dh)]
        vh = kv_ref[:, pl.ds(d + h * dh, dh)]
        s = _dot_nt(q[:, h * dh:(h + 1) * dh], kh) * (dh ** -0.5)
        e = jnp.exp(s - jnp.max(s, axis=-1, keepdims=True))
        p = e * (1.0 / jnp.sum(e, axis=-1, keepdims=True))
        heads.append(_dot(p.astype(BF16), vh).astype(BF16))
    return x + _dot(jnp.concatenate(heads, axis=-1), wo_ref[...])


def _xattn_kernel(x_ref, g_ref, wq_ref, kv_ref, wo_ref, o_ref):
    o_ref[...] = _xattn_rows(x_ref[...], g_ref, wq_ref, kv_ref, wo_ref)


def _mix0_xattn_kernel(o_ref, y3_ref, x_ref, wglu_ref, bglu_ref, wout_ref, g_ref, wq_ref, kv_ref, wo_ref,
                       out_ref):
    x = _mix0_rows(o_ref, y3_ref, x_ref[...], wglu_ref, bglu_ref, wout_ref)
    out_ref[...] = _xattn_rows(x, g_ref, wq_ref, kv_ref, wo_ref)


def xattn(x, kv, gain, wq, wo, batch, seq, tm, mix0=None):
    t, d = x.shape
    n_mem = kv.shape[0] // batch
    ns = seq // tm
    row = pl.BlockSpec((tm, d), lambda b, s: (b * ns + s, 0))
    attn_specs = [_resident((1, d)), _resident(wq.shape), pl.BlockSpec((n_mem, 2 * d), lambda b, s: (b, 0)),
                  _resident(wo.shape)]
    attn_args = (gain.reshape(1, d), wq, kv, wo)
    if mix0 is None:
        body, in_specs, args = _xattn_kernel, [row] + attn_specs, (x,) + attn_args
    else:
        o, y3, w_glu, b_glu, w_out = mix0
        half = o.shape[1]
        body = _mix0_xattn_kernel
        in_specs = [pl.BlockSpec((tm, half), lambda b, s: (b * ns + s, 0)),
                    pl.BlockSpec((tm // LANES, half, LANES), lambda b, s: (b * ns + s, 0, 0)), row,
                    _resident(w_glu.shape), _resident((1, half)), _resident(w_out.shape)] + attn_specs
        args = (o, y3, x, w_glu, b_glu.reshape(1, half), w_out) + attn_args
    return pl.pallas_call(
        body, grid=(batch, ns), in_specs=in_specs, out_specs=row,
        out_shape=jax.ShapeDtypeStruct((t, d), F32),
        compiler_params=_params("parallel", "parallel"), name="xattn",
    )(*args)


def _ffn_kernel(x_ref, g_ref, wup_ref, cw_ref, wd_ref, fg_ref, o_ref, xn_ref, tail_ref, act_ref,
                *, final_norm):
    tm = x_ref.shape[0]
    f = wd_ref.shape[0]
    fc = FFN_COLS
    pad = SUBLANES

    @pl.when(pl.program_id(1) == 0)
    def _():
        tail_ref[...] = jnp.zeros_like(tail_ref)

    xn_ref[...] = _rms(x_ref[...], g_ref[...]).astype(BF16)

    def proj_conv(col0):
        cols = pl.ds(col0, fc)
        cur = _dot(xn_ref[...], wup_ref[:, cols])
        z = jnp.concatenate([tail_ref[:, cols], cur], axis=0)
        tail_ref[:, cols] = cur[tm - pad:, :]
        y = cw_ref[0:1, cols] * z
        y = cw_ref[1:2, cols] * z + pltpu.roll(y, 1, 0)
        y = cw_ref[2:3, cols] * z + pltpu.roll(y, 1, 0)
        return y[pad:, :]

    for cb in range(f // fc):
        up = proj_conv(cb * fc)
        gate = proj_conv(f + cb * fc)
        act_ref[:, pl.ds(cb * fc, fc)] = (_silu(gate) * up).astype(BF16)
    y = x_ref[...] + _dot(act_ref[...], wd_ref[...])
    if final_norm:
        y = _rms(y, fg_ref[...])
    o_ref[...] = y


def conv_ffn(x, gain, w_up, conv_w, w_down, final_gain, batch, seq, tm):
    t, d = x.shape
    f = w_down.shape[0]
    assert FFN_CONV == conv_w.shape[0] == 3 and f % FFN_COLS == 0
    ns = seq // tm
    final_norm = final_gain is not None
    fg = (final_gain if final_norm else jnp.ones((d,), F32)).reshape(1, d)
    row = pl.BlockSpec((tm, d), lambda b, s: (b * ns + s, 0))
    return pl.pallas_call(
        functools.partial(_ffn_kernel, final_norm=final_norm), grid=(batch, ns),
        in_specs=[row, _resident((1, d)), _resident(w_up.shape), _resident(conv_w.shape),
                  _resident(w_down.shape), _resident((1, d))],
        out_specs=row, out_shape=jax.ShapeDtypeStruct((t, d), F32),
        scratch_shapes=[pltpu.VMEM((tm, d), BF16), pltpu.VMEM((SUBLANES, 2 * f), F32),
                        pltpu.VMEM((tm, f), BF16)],
        compiler_params=_params("parallel", "arbitrary"), name="conv_ffn",
    )(x, gain.reshape(1, d), w_up, conv_w, w_down, fg)


def _unit_lower_inverses(a_list, eye):
    n = eye.shape[0]
    ms = [eye - a for a in a_list]
    ps = [(-a).astype(BF16) for a in a_list]
    ps = [_dot(p, p).astype(BF16) for p in ps]
    for _ in range(4):
        both = [_dot(jnp.concatenate([m.astype(BF16), p], axis=0), p) for m, p in zip(ms, ps)]
        ms = [m + b[0:n] for m, b in zip(ms, both)]
        ps = [b[n:].astype(BF16) for b in both]
    return [m + _dot(m.astype(BF16), p) for m, p in zip(ms, ps)]


def _gdn_proj_kernel(x_ref, g_ref, wqkv_ref, wz_ref, wg_ref, cw_ref, qkv_ref, z_ref, gates_ref,
                     xn_ref, tail_ref):
    tm = x_ref.shape[0]
    width = GDN_HEADS * GDN_HEAD_DIM
    dh = GDN_HEAD_DIM
    cb_cols = 2 * dh
    pad = SUBLANES

    @pl.when(pl.program_id(1) == 0)
    def _():
        tail_ref[...] = jnp.zeros_like(tail_ref)

    xn_ref[...] = _rms(x_ref[...], g_ref[...]).astype(BF16)
    for cb in range(3 * width // cb_cols):
        cols = pl.ds(cb * cb_cols, cb_cols)
        cur = _dot(xn_ref[...], wqkv_ref[:, cols])
        zz = jnp.concatenate([tail_ref[:, cols], cur], axis=0)
        tail_ref[:, cols] = cur[tm - pad:, :]
        zs = pltpu.roll(zz, 1, 0)
        near = cw_ref[3:4, cols] * zz + cw_ref[2:3, cols] * zs
        far = cw_ref[1:2, cols] * zz + cw_ref[0:1, cols] * zs
        y = _silu((near + pltpu.roll(far, 2, 0))[pad:, :])
        if cb * cb_cols < 2 * width:
            scale = dh ** -0.5 if cb * cb_cols < width else 1.0
            parts = []
            for j in range(cb_cols // dh):
                yh = y[:, j * dh:(j + 1) * dh]
                parts.append(yh * (lax.rsqrt(jnp.sum(yh * yh, axis=-1, keepdims=True) + EPS) * scale))
            y = jnp.concatenate(parts, axis=1)
        qkv_ref[:, cols] = y.astype(qkv_ref.dtype)
        if cb % 3 == 2:
            zc = pl.ds((cb // 3) * cb_cols, cb_cols)
            z_ref[:, zc] = _dot(xn_ref[...], wz_ref[:, zc]).astype(z_ref.dtype)
        if cb == 0:
            gates_ref[...] = _dot(xn_ref[...], wg_ref[...])


def gdn_proj(x, gain, w_qkv, w_z, w_gates, conv_w, batch, seq, tm):
    t, d = x.shape
    ns = seq // tm
    assert conv_w.shape[0] == GDN_CONV

    def row(w):
        return pl.BlockSpec((tm, w), lambda b, s: (b * ns + s, 0))

    widths = [w_qkv.shape[1], w_z.shape[1], w_gates.shape[1]]
    return pl.pallas_call(
        _gdn_proj_kernel, grid=(batch, ns),
        in_specs=[row(d), _resident((1, d)), _resident(w_qkv.shape), _resident(w_z.shape),
                  _resident(w_gates.shape), _resident(conv_w.shape)],
        out_specs=[row(w) for w in widths],
        out_shape=[jax.ShapeDtypeStruct((t, w), dt) for w, dt in zip(widths, (BF16, BF16, F32))],
        scratch_shapes=[pltpu.VMEM((tm, d), BF16), pltpu.VMEM((SUBLANES, w_qkv.shape[1]), F32)],
        compiler_params=_params("parallel", "arbitrary"), name="gdn_proj",
    )(x, gain.reshape(1, d), w_qkv, w_z, w_gates, conv_w)


def _gdn_kernel(qkv_ref, z_ref, b_ref, a_ref, x_ref, alog_ref, dtb_ref, on_ref, wout_ref, o_ref,
                q_ref, k_ref, v_ref, w_ref, qk_ref, gc_ref, gct_ref, beta_ref, ob_ref, state_ref):
    ts = qkv_ref.shape[0]
    width = GDN_HEADS * GDN_HEAD_DIM
    dh = GDN_HEAD_DIM
    c = GDN_CHUNK
    heads = range(GDN_HEADS)

    @pl.when(pl.program_id(1) == 0)
    def _():
        state_ref[...] = jnp.zeros_like(state_ref)

    beta_ref[...] = jax.nn.sigmoid(b_ref[...])
    ag = a_ref[...] + dtb_ref[...]
    softplus = jnp.maximum(ag, 0.0) + jnp.log1p(jnp.exp(-jnp.abs(ag)))
    gc = -(jnp.exp(alog_ref[...]) * softplus)
    pos = lax.broadcasted_iota(jnp.int32, (ts, LANES), 0) % c
    shift = 1
    while shift < c:
        gc = gc + jnp.where(pos >= shift, pltpu.roll(gc, shift, 0), 0.0)
        shift *= 2
    gc_ref[...] = gc
    gct_ref[...] = gc.T

    ri = lax.broadcasted_iota(jnp.int32, (2 * c, 2 * c), 0)
    ci = lax.broadcasted_iota(jnp.int32, (2 * c, 2 * c), 1)
    same = (ri // c) == (ci // c)
    incl = same & (ri >= ci)
    strict = same & (ri > ci)
    eye = jnp.where(ri == ci, 1.0, 0.0).astype(F32)

    def hcols(h):
        return pl.ds(h * dh, dh)

    def wy_factors(dc):
        r0 = dc * 2 * c
        rows = pl.ds(r0, 2 * c)
        gcd = gc_ref[rows, :]
        gctd = gct_ref[:, rows]
        bet = beta_ref[rows, :]
        qbf = [qkv_ref[rows, hcols(h)] for h in heads]
        kbf = [qkv_ref[rows, hcols(GDN_HEADS + h)] for h in heads]
        qs = [q.astype(F32) for q in qbf]
        ks = [k.astype(F32) for k in kbf]
        vs = [qkv_ref[rows, hcols(2 * GDN_HEADS + h)].astype(F32) for h in heads]
        gcols = [gcd[:, h:h + 1] for h in heads]
        bcols = [bet[:, h:h + 1] for h in heads]
        decays = [jnp.where(incl, jnp.exp(jnp.where(incl, gcols[h] - gctd[h:h + 1, :], 0.0)), 0.0)
                  for h in heads]
        kbs = [ks[h] * bcols[h] for h in heads]
        both = [_dot_nt(jnp.concatenate([kbs[h].astype(BF16), qbf[h]], axis=0), kbf[h]) for h in heads]
        a_mats = [jnp.where(strict, both[h][0:2 * c] * decays[h], 0.0) for h in heads]
        for h in heads:
            qk_ref[rows, hcols(h)] = jnp.where(incl, both[h][2 * c:] * decays[h], 0.0).astype(BF16)
        t_mats = [t.astype(BF16) for t in _unit_lower_inverses(a_mats, eye)]
        egs = [jnp.exp(g) for g in gcols]
        for h in heads:
            w_ref[rows, hcols(h)] = _dot(t_mats[h], (kbs[h] * egs[h]).astype(BF16)).astype(BF16)
        for h in heads:
            v_ref[rows, hcols(h)] = _dot(t_mats[h], (vs[h] * bcols[h]).astype(BF16))
        for h in heads:
            q_ref[rows, hcols(h)] = (qs[h] * egs[h]).astype(BF16)
            gl0 = gcd[c - 1:c, h:h + 1]
            gl1 = gcd[2 * c - 1:2 * c, h:h + 1]
            glcol = jnp.concatenate([jnp.broadcast_to(gl0, (c, 1)), jnp.broadcast_to(gl1, (c, 1))], axis=0)
            k_ref[rows, hcols(h)] = (ks[h] * jnp.exp(glcol - gcols[h])).astype(BF16)

    def recurrence(dc):
        r0 = dc * 2 * c
        rows = pl.ds(r0, 2 * c)
        sts = [state_ref[h] for h in heads]
        vns, outs = [], []
        for half in range(2):
            r = pl.ds(r0 + half * c, c)
            egl = jnp.exp(gc_ref[pl.ds(r0 + half * c + c - 1, 1), :])
            sbs = [s.astype(BF16) for s in sts]
            ws = [_dot(jnp.concatenate([w_ref[r, hcols(h)], q_ref[r, hcols(h)]], axis=0), sbs[h]) for h in heads]
            vn = [v_ref[r, hcols(h)] - ws[h][0:c] for h in heads]
            outs.append([ws[h][c:] for h in heads])
            sts = [sts[h] * egl[:, h:h + 1] + _dot_tn(k_ref[r, hcols(h)], vn[h].astype(BF16))
                   for h in heads]
            vns.append(vn)
        for h in heads:
            state_ref[h] = sts[h]
        for h in heads:
            vn = jnp.concatenate([vns[0][h], vns[1][h]], axis=0).astype(BF16)
            o = jnp.concatenate([outs[0][h], outs[1][h]], axis=0) + _dot(qk_ref[rows, hcols(h)], vn)
            o = o * lax.rsqrt(jnp.mean(o * o, axis=-1, keepdims=True) + EPS) * on_ref[...]
            ob_ref[rows, hcols(h)] = (o * _silu(z_ref[rows, hcols(h)].astype(F32))).astype(BF16)

    n_pairs = ts // (2 * c)
    wy_factors(0)
    for dc in range(n_pairs):
        if dc + 1 < n_pairs:
            wy_factors(dc + 1)
        recurrence(dc)
    o_ref[...] = x_ref[...] + _dot(ob_ref[...], wout_ref[...])


def gated_deltanet(qkv, z, gates, x, a_log, dt_bias, o_norm, w_out, batch, seq, ts):
    t, d = x.shape
    width = GDN_HEADS * GDN_HEAD_DIM
    ns = seq // ts

    def lane_pad(vec):
        return jnp.pad(vec, (0, LANES - vec.shape[0])).reshape(1, LANES)

    def row(w, j):
        return pl.BlockSpec((ts, w), lambda b, s: (b * ns + s, j))

    return pl.pallas_call(
        _gdn_kernel, grid=(batch, ns),
        in_specs=[row(3 * width, 0), row(width, 0), row(LANES, 0), row(LANES, 1), row(d, 0),
                  _resident((1, LANES)), _resident((1, LANES)), _resident((1, GDN_HEAD_DIM)),
                  _resident(w_out.shape)],
        out_specs=row(d, 0), out_shape=jax.ShapeDtypeStruct((t, d), F32),
        scratch_shapes=[pltpu.VMEM((ts, width), BF16), pltpu.VMEM((ts, width), BF16),
                        pltpu.VMEM((ts, width), F32),
                        pltpu.VMEM((ts, width), BF16), pltpu.VMEM((ts, width), BF16),
                        pltpu.VMEM((ts, LANES), F32), pltpu.VMEM((LANES, ts), F32),
                        pltpu.VMEM((ts, LANES), F32), pltpu.VMEM((ts, width), BF16),
                        pltpu.VMEM((GDN_HEADS, GDN_HEAD_DIM, GDN_HEAD_DIM), F32)],
        compiler_params=_params("parallel", "arbitrary"), name="gated_deltanet",
    )(qkv, z, gates, gates, x, lane_pad(a_log), lane_pad(dt_bias), o_norm.reshape(1, GDN_HEAD_DIM), w_out)


def _common_block(x, mem, xa_norm, mem_norm, xa_wq, xa_wkv, xa_wo, ffn_norm, ffn_w_up, ffn_conv,
                  ffn_w_down, final_gain, batch, seq, tm, mix0=None):
    (kv,) = norm_proj(mem, mem_norm, [xa_wkv.astype(BF16)], [BF16], tm=min(tm, mem.shape[0]))
    x = xattn(x, kv, xa_norm, xa_wq.astype(BF16), xa_wo.astype(BF16), batch, seq, tm, mix0)
    return conv_ffn(x, ffn_norm, ffn_w_up.astype(BF16), ffn_conv, ffn_w_down.astype(BF16), final_gain,
                    batch, seq, tm)


def kernel(x, mem, l0_mix_norm, l0_w_in, l0_ret_norm, l0_s5_lambda_re, l0_s5_lambda_im, l0_s5_b_re, l0_s5_b_im, l0_s5_c_re, l0_s5_c_im, l0_s5_d, l0_s5_log_dt, l0_s5_w_glu, l0_s5_b_glu, l0_w_out, l0_xa_norm, l0_mem_norm, l0_xa_wq, l0_xa_wkv, l0_xa_wo, l0_ffn_norm, l0_ffn_w_up, l0_ffn_conv, l0_ffn_w_down, l1_mix_norm, l1_w_in, l1_conv, l1_a_log, l1_dt_bias, l1_o_norm, l1_w_out, l1_xa_norm, l1_mem_norm, l1_xa_wq, l1_xa_wkv, l1_xa_wo, l1_ffn_norm, l1_ffn_w_up, l1_ffn_conv, l1_ffn_w_down, final_norm):
    batch, seq, d = x.shape
    tm = min(ROW_TILE, seq)
    xf = x.reshape(batch * seq, d)
    memf = mem.reshape(-1, d)
    ret_width = RET_HEADS * RET_HEAD_DIM

    o_ret, u3 = even_mixer_in(xf, l0_mix_norm, l0_w_in[:, :4 * ret_width].astype(BF16),
                              l0_w_in[:, 4 * ret_width:].T.astype(BF16), l0_ret_norm, batch, seq, tm)
    y3 = s5_ssm(u3, l0_s5_lambda_re, l0_s5_lambda_im, l0_s5_b_re, l0_s5_b_im, l0_s5_c_re, l0_s5_c_im,
                l0_s5_d, l0_s5_log_dt, batch)
    mix0 = (o_ret, y3, l0_s5_w_glu.astype(BF16), l0_s5_b_glu, l0_w_out.astype(BF16))
    xf = _common_block(xf, memf, l0_xa_norm, l0_mem_norm, l0_xa_wq, l0_xa_wkv, l0_xa_wo, l0_ffn_norm,
                       l0_ffn_w_up, l0_ffn_conv, l0_ffn_w_down, None, batch, seq, tm, mix0)

    gdn_width = GDN_HEADS * GDN_HEAD_DIM
    w_qkv = l1_w_in[:, :3 * gdn_width].astype(BF16)
    w_z = l1_w_in[:, 3 * gdn_width:4 * gdn_width].astype(BF16)
    w_b = jnp.pad(l1_w_in[:, 4 * gdn_width:4 * gdn_width + GDN_HEADS], ((0, 0), (0, LANES - GDN_HEADS)))
    w_a = jnp.pad(l1_w_in[:, 4 * gdn_width + GDN_HEADS:], ((0, 0), (0, LANES - GDN_HEADS)))
    w_gates = jnp.concatenate([w_b, w_a], axis=1).astype(BF16)
    qkv, z, gates = gdn_proj(xf, l1_mix_norm, w_qkv, w_z, w_gates, l1_conv, batch, seq, tm)
    xf = gated_deltanet(qkv, z, gates, xf, l1_a_log, l1_dt_bias, l1_o_norm, l1_w_out.astype(BF16),
                        batch, seq, min(GDN_ROW_TILE, tm))
    xf = _common_block(xf, memf, l1_xa_norm, l1_mem_norm, l1_xa_wq, l1_xa_wkv, l1_xa_wo, l1_ffn_norm,
                       l1_ffn_w_up, l1_ffn_conv, l1_ffn_w_down, final_norm, batch, seq, tm)
    return xf.reshape(batch, seq, d)
```

```python
import functools
import math

import jax
import jax.numpy as jnp
from jax import lax
from jax.experimental import pallas as pl
from jax.experimental.pallas import tpu as pltpu

F32 = jnp.float32
BF16 = jnp.bfloat16
EPS = 1e-6

V7X_VMEM_BYTES = 64 * 1024 * 1024
VMEM_LIMIT_BYTES = V7X_VMEM_BYTES - 8 * 1024 * 1024
LANES = 128
SUBLANES = 8

RET_HEADS = 4
RET_HEAD_DIM = 128
RET_CHUNK = 128
ROPE_BASE = 10000.0
S5_GROUP = 16
S5_STATE = 64
S5_CHUNK = 64
GDN_HEADS = 8
GDN_HEAD_DIM = 128
GDN_CONV = 4
GDN_CHUNK = 64
XA_HEADS = 4
FFN_CONV = 3
FFN_COLS = 256
ROW_TILE = 1024
GDN_VMEM_LIMIT_BYTES = V7X_VMEM_BYTES - 4 * 1024 * 1024

HIGHEST = lax.Precision.HIGHEST


def _params(*semantics, vmem_limit_bytes=VMEM_LIMIT_BYTES):
    return pltpu.CompilerParams(dimension_semantics=semantics, vmem_limit_bytes=vmem_limit_bytes)


def _resident(shape):
    nd = len(shape)
    return pl.BlockSpec(shape, lambda *_: (0,) * nd)


def _dot(a, b, precision=None):
    return jnp.dot(a, b, preferred_element_type=F32, precision=precision)


def _dot_nt(a, b):
    return lax.dot_general(a, b, (((1,), (1,)), ((), ())), preferred_element_type=F32)


def _dot_tn(a, b):
    return lax.dot_general(a, b, (((0,), (0,)), ((), ())), preferred_element_type=F32)


def _rms(x, g):
    return x * lax.rsqrt(jnp.mean(x * x, axis=-1, keepdims=True) + EPS) * g


def _silu(x):
    h = 0.5 * x
    return h + h * jnp.tanh(h)


def _norm_proj_kernel(x_ref, g_ref, *refs):
    n_out = len(refs) // 2
    xn = _rms(x_ref[...], g_ref[...]).astype(BF16)
    for w_ref, o_ref in zip(refs[:n_out], refs[n_out:]):
        o_ref[...] = _dot(xn, w_ref[...]).astype(o_ref.dtype)


def norm_proj(x, gain, weights, out_dtypes, tm):
    t, d = x.shape
    in_specs = [pl.BlockSpec((tm, d), lambda i: (i, 0)), _resident((1, d))]
    in_specs += [_resident(w.shape) for w in weights]
    out_specs = [pl.BlockSpec((tm, w.shape[1]), lambda i: (i, 0)) for w in weights]
    out_shape = [jax.ShapeDtypeStruct((t, w.shape[1]), dt) for w, dt in zip(weights, out_dtypes)]
    return pl.pallas_call(
        _norm_proj_kernel, grid=(t // tm,), in_specs=in_specs, out_specs=out_specs,
        out_shape=out_shape, compiler_params=_params("parallel"), name="norm_proj",
    )(x, gain.reshape(1, d), *weights)


def _even_mixer_kernel(x_ref, g_ref, w_ref, wut_ref, cos_ref, sin_ref, intra_ref, qdec_ref, kdec_ref,
                       cdec_ref, rn_ref, o_ref, u3_ref, proj_ref, state_ref):
    c = RET_CHUNK
    width = RET_HEADS * RET_HEAD_DIM

    @pl.when(pl.program_id(1) == 0)
    def _():
        state_ref[...] = jnp.zeros_like(state_ref)

    xn = _rms(x_ref[...], g_ref[...]).astype(BF16)
    ut = _dot_nt(wut_ref[...], xn)
    for k in range(u3_ref.shape[0]):
        u3_ref[k] = ut[:, k * LANES:(k + 1) * LANES]
    proj_ref[...] = _dot(xn, w_ref[...]).astype(proj_ref.dtype)

    scale = RET_HEAD_DIM ** -0.5
    for ci in range(x_ref.shape[0] // c):
        rows = pl.ds(ci * c, c)
        cos = cos_ref[rows, :]
        sin = sin_ref[rows, :]
        for h in range(RET_HEADS):
            cols = pl.ds(h * RET_HEAD_DIM, RET_HEAD_DIM)

            def part(j):
                return proj_ref[rows, pl.ds(j * width + h * RET_HEAD_DIM, RET_HEAD_DIM)]

            qh = part(0).astype(F32)
            kh = part(1).astype(F32)
            vh = part(2)
            qr = qh * cos + pltpu.roll(qh, RET_HEAD_DIM // 2, 1) * sin
            kr = (kh * cos + pltpu.roll(kh, RET_HEAD_DIM // 2, 1) * sin) * scale
            scores = _dot_nt(qr.astype(BF16), kr.astype(BF16)) * intra_ref[h]
            inner = _dot(scores.astype(BF16), vh)
            st = state_ref[h]
            cross = _dot((qr * qdec_ref[:, cols]).astype(BF16), st.astype(BF16))
            kv = _dot_tn((kr * kdec_ref[:, cols]).astype(BF16), vh)
            state_ref[h] = st * cdec_ref[:, cols] + kv
            o = inner + cross
            o = o * lax.rsqrt(jnp.mean(o * o, axis=-1, keepdims=True) + EPS)
            o_ref[rows, cols] = (o * rn_ref[:, cols] * _silu(part(3).astype(F32))).astype(o_ref.dtype)


def even_mixer_in(x, gain, w, w_u_t, ret_norm, batch, seq, ts):
    t, d = x.shape
    nu = w_u_t.shape[0]
    width = RET_HEADS * RET_HEAD_DIM
    c = RET_CHUNK
    half = RET_HEAD_DIM // 2
    ns = seq // ts
    inv = jnp.exp(-math.log(ROPE_BASE) * jnp.arange(half, dtype=F32) / half)
    ang = jnp.arange(seq, dtype=F32)[:, None] * inv[None, :]
    cos = jnp.concatenate([jnp.cos(ang), jnp.cos(ang)], axis=-1)
    sin = jnp.concatenate([-jnp.sin(ang), jnp.sin(ang)], axis=-1)
    log_gamma = jnp.log1p(-jnp.exp2(-5.0 - jnp.arange(RET_HEADS, dtype=F32)))
    idx = jnp.arange(c, dtype=F32)
    diff = idx[:, None] - idx[None, :]
    causal = diff >= 0
    intra = jnp.where(causal, jnp.exp(log_gamma[:, None, None] * jnp.where(causal, diff, 0.0)), 0.0)

    def per_head_cols(tab):
        return jnp.repeat(tab.T, RET_HEAD_DIM, axis=1)

    qdec = per_head_cols(jnp.exp(log_gamma[:, None] * (idx + 1)))
    kdec = per_head_cols(jnp.exp(log_gamma[:, None] * (c - 1 - idx)))
    cdec = per_head_cols(jnp.exp(log_gamma * c)[:, None])

    in_specs = [pl.BlockSpec((ts, d), lambda b, s: (b * ns + s, 0)), _resident((1, d)), _resident(w.shape),
                _resident(w_u_t.shape),
                pl.BlockSpec((ts, RET_HEAD_DIM), lambda b, s: (s, 0)),
                pl.BlockSpec((ts, RET_HEAD_DIM), lambda b, s: (s, 0)),
                _resident(intra.shape), _resident(qdec.shape), _resident(kdec.shape),
                _resident(cdec.shape), _resident((1, width))]
    return pl.pallas_call(
        _even_mixer_kernel, grid=(batch, ns), in_specs=in_specs,
        out_specs=[pl.BlockSpec((ts, width), lambda b, s: (b * ns + s, 0)),
                   pl.BlockSpec((ts // LANES, nu, LANES), lambda b, s: (b * ns + s, 0, 0))],
        out_shape=[jax.ShapeDtypeStruct((t, width), BF16), jax.ShapeDtypeStruct((t // LANES, nu, LANES), F32)],
        scratch_shapes=[pltpu.VMEM((ts, w.shape[1]), BF16),
                        pltpu.VMEM((RET_HEADS, RET_HEAD_DIM, RET_HEAD_DIM), F32)],
        compiler_params=_params("parallel", "arbitrary"), name="even_mixer_in",
    )(x, gain.reshape(1, d), w, w_u_t, cos, sin, intra, qdec, kdec, cdec, ret_norm.reshape(1, width))


def _s5_prep_kernel(lr_ref, li_ref, ldt_ref, lrc_ref, lic_ref, ldtc_ref, br_ref, bi_ref, cr_ref, ci_ref,
                    ktoe_ref, bdr_ref, bdi_ref, cdr_ref, cdi_ref, al_ref, *, chunk):
    hg = S5_GROUP
    n = chunk * hg
    lr, li = lr_ref[...], li_ref[...]
    dt = jnp.exp(ldt_ref[...])
    mag = jnp.exp(lr * dt)
    a_re = mag * jnp.cos(li * dt)
    a_im = mag * jnp.sin(li * dt)
    den = lr * lr + li * li
    z_re = ((a_re - 1.0) * lr + a_im * li) / den
    z_im = (a_im * lr - (a_re - 1.0) * li) / den

    def zoh(br, bi):
        return z_re * br - z_im * bi, z_re * bi + z_im * br

    e = (chunk - 1.0) - lax.broadcasted_iota(jnp.int32, (chunk, S5_STATE), 0).astype(F32)
    pm = jnp.exp(lr * dt * e)
    pr = jnp.concatenate([pm * jnp.cos(li * dt * e)] * hg, axis=0)
    pi = jnp.concatenate([pm * jnp.sin(li * dt * e)] * hg, axis=0)
    b16_re, b16_im = zoh(br_ref[...], bi_ref[...])

    def over_positions(b):
        return jnp.broadcast_to(b[:, None, :], (hg, chunk, S5_STATE)).reshape(n, S5_STATE)

    bb_re, bb_im = over_positions(b16_re), over_positions(b16_im)
    bdr_ref[...] = (bb_re * pr - bb_im * pi).astype(bdr_ref.dtype)
    bdi_ref[...] = (bb_re * pi + bb_im * pr).astype(bdi_ref.dtype)
    lm = jnp.exp(lr * dt * chunk)
    al_ref[0:1, :] = lm * jnp.cos(li * dt * chunk)
    al_ref[1:2, :] = lm * jnp.sin(li * dt * chunk)

    lrc, lic = lrc_ref[...], lic_ref[...]
    dtc = jnp.exp(ldtc_ref[...])
    tau = lax.broadcasted_iota(jnp.int32, (S5_STATE, chunk), 1).astype(F32)
    vm = jnp.exp(lrc * dtc * tau)
    vc = jnp.concatenate([vm * jnp.cos(lic * dtc * tau)] * hg, axis=1)
    vs = jnp.concatenate([vm * jnp.sin(lic * dtc * tau)] * hg, axis=1)

    def over_lanes(cm):
        return jnp.concatenate([jnp.broadcast_to(cm[:, ho:ho + 1], (S5_STATE, chunk)) for ho in range(hg)], axis=1)

    crr, cri = over_lanes(cr_ref[...]), over_lanes(ci_ref[...])
    v_re = vc * crr - vs * cri
    v_im = vc * cri + vs * crr
    magc = jnp.exp(lrc * dtc)
    ac_re = magc * jnp.cos(lic * dtc)
    ac_im = magc * jnp.sin(lic * dtc)
    cdr_ref[...] = (v_re * ac_re - v_im * ac_im).astype(cdr_ref.dtype)
    cdi_ref[...] = (-(v_re * ac_im + v_im * ac_re)).astype(cdi_ref.dtype)

    p = _dot(b16_re, v_re, HIGHEST) - _dot(b16_im, v_im, HIGHEST)
    row = lax.broadcasted_iota(jnp.int32, (chunk, n), 0)
    keep = lax.broadcasted_iota(jnp.int32, (chunk, n), 1) % chunk >= row
    for hi in range(hg):
        blk = pltpu.roll(jnp.broadcast_to(p[hi:hi + 1, :], (chunk, n)), 0, 1, stride=1, stride_axis=0)
        ktoe_ref[hi * chunk:(hi + 1) * chunk, :] = jnp.where(keep, blk, 0.0).astype(ktoe_ref.dtype)


def _s5_main_kernel(u_ref, ktoe_ref, bdr_ref, bdi_ref, cdr_ref, cdi_ref, al_ref, d_ref, y_ref,
                    sre_ref, sim_ref, *, batch):
    r2 = u_ref.shape[0]
    per_seq = r2 // batch
    c = S5_CHUNK
    pieces = [u_ref[:, hi, :] for hi in range(S5_GROUP)]
    lhs = jnp.concatenate([jnp.concatenate([p[:, 0:c] for p in pieces], axis=1),
                           jnp.concatenate([p[:, c:] for p in pieces], axis=1)], axis=0).astype(BF16)
    sre_ref[...] = _dot(lhs, bdr_ref[...])
    sim_ref[...] = _dot(lhs, bdi_ref[...])
    a_re = al_ref[0:1, :]
    a_im = al_ref[1:2, :]
    s_re = jnp.zeros((batch, S5_STATE), F32)
    s_im = jnp.zeros((batch, S5_STATE), F32)
    for i in range(per_seq):
        for base in (0, r2):
            rows = pl.ds(base + i, batch, stride=per_seq)
            e_re, e_im = sre_ref[rows, :], sim_ref[rows, :]
            sre_ref[rows, :] = s_re
            sim_ref[rows, :] = s_im
            s_re, s_im = a_re * s_re - a_im * s_im + e_re, a_re * s_im + a_im * s_re + e_im
    y = _dot(lhs, ktoe_ref[...])
    y += _dot(sre_ref[...].astype(BF16), cdr_ref[...])
    y += _dot(sim_ref[...].astype(BF16), cdi_ref[...])
    for ho in range(S5_GROUP):
        cols = slice(ho * c, (ho + 1) * c)
        y_ref[:, ho, :] = (jnp.concatenate([y[0:r2, cols], y[r2:, cols]], axis=1)
                           + d_ref[ho:ho + 1, :] * pieces[ho])


def s5_ssm(u3, lam_re, lam_im, b_re, b_im, c_re, c_im, d, log_dt, batch):
    r2, width, lanes = u3.shape
    g, hg, p, chunk = width // S5_GROUP, S5_GROUP, S5_STATE, S5_CHUNK
    assert lanes == LANES == 2 * chunk and r2 % batch == 0
    n = chunk * hg

    def grp(shape):
        return pl.BlockSpec((None,) + shape, lambda i: (i, 0, 0))

    ldt = jnp.broadcast_to(log_dt[:, None, None], (g, 1, p))
    prep_in = [lam_re[:, None, :], lam_im[:, None, :], ldt,
               lam_re[:, :, None], lam_im[:, :, None], jnp.swapaxes(ldt, 1, 2),
               b_re, b_im, c_re, c_im]
    prep_specs = [grp((1, p))] * 3 + [grp((p, 1))] * 3 + [grp((hg, p))] * 2 + [grp((p, hg))] * 2
    ktoe, bdr, bdi, cdr, cdi, al = pl.pallas_call(
        functools.partial(_s5_prep_kernel, chunk=chunk),
        grid=(g,), in_specs=prep_specs,
        out_specs=[grp((n, n)), grp((n, p)), grp((n, p)), grp((p, n)), grp((p, n)), grp((2, p))],
        out_shape=[jax.ShapeDtypeStruct((g, n, n), BF16),
                   jax.ShapeDtypeStruct((g, n, p), BF16), jax.ShapeDtypeStruct((g, n, p), BF16),
                   jax.ShapeDtypeStruct((g, p, n), BF16), jax.ShapeDtypeStruct((g, p, n), BF16),
                   jax.ShapeDtypeStruct((g, 2, p), F32)],
        compiler_params=_params("parallel"), name="s5_prep",
    )(*prep_in)

    channels = pl.BlockSpec((r2, hg, LANES), lambda i: (0, i, 0))
    dvec = jnp.broadcast_to(d[:, :, None], (g, hg, LANES))
    return pl.pallas_call(
        functools.partial(_s5_main_kernel, batch=batch),
        grid=(g,),
        in_specs=[channels, grp((n, n)), grp((n, p)), grp((n, p)), grp((p, n)), grp((p, n)),
                  grp((2, p)), grp((hg, LANES))],
        out_specs=channels,
        out_shape=jax.ShapeDtypeStruct(u3.shape, F32),
        scratch_shapes=[pltpu.VMEM((2 * r2, p), F32), pltpu.VMEM((2 * r2, p), F32)],
        compiler_params=_params("parallel"), name="s5_main",
    )(u3, ktoe, bdr, bdi, cdr, cdi, al, dvec)


def _mix0_rows(o_ref, y3_ref, x, wglu_ref, bglu_ref, wout_ref):
    y = jnp.concatenate([y3_ref[k].T for k in range(y3_ref.shape[0])], axis=0)
    y = 0.5 * y * (1.0 + jnp.tanh(math.sqrt(2.0 / math.pi) * (y + 0.044715 * (y * y * y))))
    y = y * jax.nn.sigmoid(_dot(y.astype(BF16), wglu_ref[...]) + bglu_ref[...])
    merged = jnp.concatenate([o_ref[...], y.astype(BF16)], axis=-1)
    return x + _dot(merged, wout_ref[...])


def _xattn_rows(x, g_ref, wq_ref, kv_ref, wo_ref):
    d = x.shape[1]
    dh = d // XA_HEADS
    xn = _rms(x, g_ref[...]).astype(BF16)
    q = _dot(xn, wq_ref[...]).astype(BF16)
    heads = []
    for h in range(XA_HEADS):
        kh = kv_ref[:, pl.ds(h * dh, dh)]
        vh = kv_ref[:, pl.ds(d + h * dh, dh)]
        s = _dot_nt(q[:, h * dh:(h + 1) * dh], kh) * (dh ** -0.5)
        e = jnp.exp(s - jnp.max(s, axis=-1, keepdims=True))
        p = e * (1.0 / jnp.sum(e, axis=-1, keepdims=True))
        heads.append(_dot(p.astype(BF16), vh).astype(BF16))
    return x + _dot(jnp.concatenate(heads, axis=-1), wo_ref[...])


def _xattn_kernel(x_ref, g_ref, wq_ref, kv_ref, wo_ref, o_ref):
    o_ref[...] = _xattn_rows(x_ref[...], g_ref, wq_ref, kv_ref, wo_ref)


def _mix0_xattn_kernel(o_ref, y3_ref, x_ref, wglu_ref, bglu_ref, wout_ref, g_ref, wq_ref, kv_ref, wo_ref,
                       out_ref):
    x = _mix0_rows(o_ref, y3_ref, x_ref[...], wglu_ref, bglu_ref, wout_ref)
    out_ref[...] = _xattn_rows(x, g_ref, wq_ref, kv_ref, wo_ref)


def xattn(x, kv, gain, wq, wo, batch, seq, tm, mix0=None):
    t, d = x.shape
    n_mem = kv.shape[0] // batch
    ns = seq // tm
    row = pl.BlockSpec((tm, d), lambda b, s: (b * ns + s, 0))
    attn_specs = [_resident((1, d)), _resident(wq.shape), pl.BlockSpec((n_mem, 2 * d), lambda b, s: (b, 0)),
                  _resident(wo.shape)]
    attn_args = (gain.reshape(1, d), wq, kv, wo)
    if mix0 is None:
        body, in_specs, args = _xattn_kernel, [row] + attn_specs, (x,) + attn_args
    else:
        o, y3, w_glu, b_glu, w_out = mix0
        half = o.shape[1]
        body = _mix0_xattn_kernel
        in_specs = [pl.BlockSpec((tm, half), lambda b, s: (b * ns + s, 0)),
                    pl.BlockSpec((tm // LANES, half, LANES), lambda b, s: (b * ns + s, 0, 0)), row,
                    _resident(w_glu.shape), _resident((1, half)), _resident(w_out.shape)] + attn_specs
        args = (o, y3, x, w_glu, b_glu.reshape(1, half), w_out) + attn_args
    return pl.pallas_call(
        body, grid=(batch, ns), in_specs=in_specs, out_specs=row,
        out_shape=jax.ShapeDtypeStruct((t, d), F32),
        compiler_params=_params("parallel", "parallel"), name="xattn",
    )(*args)


def _ffn_kernel(x_ref, g_ref, wup_ref, cw_ref, wd_ref, fg_ref, o_ref, xn_ref, tail_ref, act_ref,
                *, final_norm):
    tm = x_ref.shape[0]
    f = wd_ref.shape[0]
    fc = FFN_COLS
    pad = SUBLANES

    @pl.when(pl.program_id(1) == 0)
    def _():
        tail_ref[...] = jnp.zeros_like(tail_ref)

    xn_ref[...] = _rms(x_ref[...], g_ref[...]).astype(BF16)

    def proj_conv(col0):
        cols = pl.ds(col0, fc)
        cur = _dot(xn_ref[...], wup_ref[:, cols])
        z = jnp.concatenate([tail_ref[:, cols], cur], axis=0)
        tail_ref[:, cols] = cur[tm - pad:, :]
        y = cw_ref[0:1, cols] * z
        y = cw_ref[1:2, cols] * z + pltpu.roll(y, 1, 0)
        y = cw_ref[2:3, cols] * z + pltpu.roll(y, 1, 0)
        return y[pad:, :]

    for cb in range(f // fc):
        up = proj_conv(cb * fc)
        gate = proj_conv(f + cb * fc)
        act_ref[:, pl.ds(cb * fc, fc)] = (_silu(gate) * up).astype(BF16)
    y = x_ref[...] + _dot(act_ref[...], wd_ref[...])
    if final_norm:
        y = _rms(y, fg_ref[...])
    o_ref[...] = y


def conv_ffn(x, gain, w_up, conv_w, w_down, final_gain, batch, seq, tm):
    t, d = x.shape
    f = w_down.shape[0]
    assert FFN_CONV == conv_w.shape[0] == 3 and f % FFN_COLS == 0
    ns = seq // tm
    final_norm = final_gain is not None
    fg = (final_gain if final_norm else jnp.ones((d,), F32)).reshape(1, d)
    row = pl.BlockSpec((tm, d), lambda b, s: (b * ns + s, 0))
    return pl.pallas_call(
        functools.partial(_ffn_kernel, final_norm=final_norm), grid=(batch, ns),
        in_specs=[row, _resident((1, d)), _resident(w_up.shape), _resident(conv_w.shape),
                  _resident(w_down.shape), _resident((1, d))],
        out_specs=row, out_shape=jax.ShapeDtypeStruct((t, d), F32),
        scratch_shapes=[pltpu.VMEM((tm, d), BF16), pltpu.VMEM((SUBLANES, 2 * f), F32),
                        pltpu.VMEM((tm, f), BF16)],
        compiler_params=_params("parallel", "arbitrary"), name="conv_ffn",
    )(x, gain.reshape(1, d), w_up, conv_w, w_down, fg)


def _unit_lower_inverses(a_list, eye):
    n = eye.shape[0]
    ms = [eye - a for a in a_list]
    ps = [(-a).astype(BF16) for a in a_list]
    ps = [_dot(p, p).astype(BF16) for p in ps]
    for _ in range(4):
        both = [_dot(jnp.concatenate([m.astype(BF16), p], axis=0), p) for m, p in zip(ms, ps)]
        ms = [m + b[0:n] for m, b in zip(ms, both)]
        ps = [b[n:].astype(BF16) for b in both]
    return [m + _dot(m.astype(BF16), p) for m, p in zip(ms, ps)]


def _gdn_proj_kernel(x_ref, g_ref, wqkv_ref, wz_ref, wg_ref, cw_ref, qkv_ref, z_ref, gates_ref,
                     xn_ref, tail_ref):
    tm = x_ref.shape[0]
    width = GDN_HEADS * GDN_HEAD_DIM
    dh = GDN_HEAD_DIM
    cb_cols = 2 * dh
    pad = SUBLANES

    @pl.when(pl.program_id(1) == 0)
    def _():
        tail_ref[...] = jnp.zeros_like(tail_ref)

    xn_ref[...] = _rms(x_ref[...], g_ref[...]).astype(BF16)
    for cb in range(3 * width // cb_cols):
        cols = pl.ds(cb * cb_cols, cb_cols)
        cur = _dot(xn_ref[...], wqkv_ref[:, cols])
        zz = jnp.concatenate([tail_ref[:, cols], cur], axis=0)
        tail_ref[:, cols] = cur[tm - pad:, :]
        zs = pltpu.roll(zz, 1, 0)
        near = cw_ref[3:4, cols] * zz + cw_ref[2:3, cols] * zs
        far = cw_ref[1:2, cols] * zz + cw_ref[0:1, cols] * zs
        y = _silu((near + pltpu.roll(far, 2, 0))[pad:, :])
        if cb * cb_cols < 2 * width:
            scale = dh ** -0.5 if cb * cb_cols < width else 1.0
            parts = []
            for j in range(cb_cols // dh):
                yh = y[:, j * dh:(j + 1) * dh]
                parts.append(yh * (lax.rsqrt(jnp.sum(yh * yh, axis=-1, keepdims=True) + EPS) * scale))
            y = jnp.concatenate(parts, axis=1)
        qkv_ref[:, cols] = y.astype(qkv_ref.dtype)
        if cb % 3 == 2:
            zc = pl.ds((cb // 3) * cb_cols, cb_cols)
            z_ref[:, zc] = _dot(xn_ref[...], wz_ref[:, zc]).astype(z_ref.dtype)
        if cb == 0:
            gates_ref[...] = _dot(xn_ref[...], wg_ref[...])


def gdn_proj(x, gain, w_qkv, w_z, w_gates, conv_w, batch, seq, tm):
    t, d = x.shape
    ns = seq // tm
    assert conv_w.shape[0] == GDN_CONV

    def row(w):
        return pl.BlockSpec((tm, w), lambda b, s: (b * ns + s, 0))

    widths = [w_qkv.shape[1], w_z.shape[1], w_gates.shape[1]]
    return pl.pallas_call(
        _gdn_proj_kernel, grid=(batch, ns),
        in_specs=[row(d), _resident((1, d)), _resident(w_qkv.shape), _resident(w_z.shape),
                  _resident(w_gates.shape), _resident(conv_w.shape)],
        out_specs=[row(w) for w in widths],
        out_shape=[jax.ShapeDtypeStruct((t, w), dt) for w, dt in zip(widths, (BF16, BF16, F32))],
        scratch_shapes=[pltpu.VMEM((tm, d), BF16), pltpu.VMEM((SUBLANES, w_qkv.shape[1]), F32)],
        compiler_params=_params("parallel", "arbitrary"), name="gdn_proj",
    )(x, gain.reshape(1, d), w_qkv, w_z, w_gates, conv_w)


def _gdn_kernel(qkv_ref, z_ref, b_ref, a_ref, x_ref, alog_ref, dtb_ref, on_ref, wout_ref, o_ref,
                q_ref, k_ref, v_ref, w_ref, qk_ref, gc_ref, gct_ref, beta_ref, ob_ref, state_ref):
    ts = qkv_ref.shape[0]
    width = GDN_HEADS * GDN_HEAD_DIM
    dh = GDN_HEAD_DIM
    c = GDN_CHUNK
    heads = range(GDN_HEADS)

    @pl.when(pl.program_id(1) == 0)
    def _():
        state_ref[...] = jnp.zeros_like(state_ref)

    beta_ref[...] = jax.nn.sigmoid(b_ref[...])
    ag = a_ref[...] + dtb_ref[...]
    softplus = jnp.maximum(ag, 0.0) + jnp.log1p(jnp.exp(-jnp.abs(ag)))
    gc = -(jnp.exp(alog_ref[...]) * softplus)
    pos = lax.broadcasted_iota(jnp.int32, (ts, LANES), 0) % c
    shift = 1
    while shift < c:
        gc = gc + jnp.where(pos >= shift, pltpu.roll(gc, shift, 0), 0.0)
        shift *= 2
    gc_ref[...] = gc
    gct_ref[...] = gc.T

    ri = lax.broadcasted_iota(jnp.int32, (2 * c, 2 * c), 0)
    ci = lax.broadcasted_iota(jnp.int32, (2 * c, 2 * c), 1)
    same = (ri // c) == (ci // c)
    incl = same & (ri >= ci)
    strict = same & (ri > ci)
    eye = jnp.where(ri == ci, 1.0, 0.0).astype(F32)

    def hcols(h):
        return pl.ds(h * dh, dh)

    def wy_factors(dc):
        r0 = dc * 2 * c
        rows = pl.ds(r0, 2 * c)
        gcd = gc_ref[rows, :]
        gctd = gct_ref[:, rows]
        bet = beta_ref[rows, :]
        qbf = [qkv_ref[rows, hcols(h)] for h in heads]
        kbf = [qkv_ref[rows, hcols(GDN_HEADS + h)] for h in heads]
        qs = [q.astype(F32) for q in qbf]
        ks = [k.astype(F32) for k in kbf]
        vs = [qkv_ref[rows, hcols(2 * GDN_HEADS + h)].astype(F32) for h in heads]
        gcols = [gcd[:, h:h + 1] for h in heads]
        bcols = [bet[:, h:h + 1] for h in heads]
        decays = [jnp.where(incl, jnp.exp(jnp.where(incl, gcols[h] - gctd[h:h + 1, :], 0.0)), 0.0)
                  for h in heads]
        kbs = [ks[h] * bcols[h] for h in heads]
        both = [_dot_nt(jnp.concatenate([kbs[h].astype(BF16), qbf[h]], axis=0), kbf[h]) for h in heads]
        a_mats = [jnp.where(strict, both[h][0:2 * c] * decays[h], 0.0) for h in heads]
        for h in heads:
            qk_ref[rows, hcols(h)] = jnp.where(incl, both[h][2 * c:] * decays[h], 0.0).astype(BF16)
        t_mats = [t.astype(BF16) for t in _unit_lower_inverses(a_mats, eye)]
        egs = [jnp.exp(g) for g in gcols]
        for h in heads:
            w_ref[rows, hcols(h)] = _dot(t_mats[h], (kbs[h] * egs[h]).astype(BF16)).astype(BF16)
        for h in heads:
            v_ref[rows, hcols(h)] = _dot(t_mats[h], (vs[h] * bcols[h]).astype(BF16))
        for h in heads:
            q_ref[rows, hcols(h)] = (qs[h] * egs[h]).astype(BF16)
            gl0 = gcd[c - 1:c, h:h + 1]
            gl1 = gcd[2 * c - 1:2 * c, h:h + 1]
            glcol = jnp.concatenate([jnp.broadcast_to(gl0, (c, 1)), jnp.broadcast_to(gl1, (c, 1))], axis=0)
            k_ref[rows, hcols(h)] = (ks[h] * jnp.exp(glcol - gcols[h])).astype(BF16)

    def recurrence(dc):
        r0 = dc * 2 * c
        rows = pl.ds(r0, 2 * c)
        sts = [state_ref[h] for h in heads]
        vns, outs = [], []
        for half in range(2):
            r = pl.ds(r0 + half * c, c)
            egl = jnp.exp(gc_ref[pl.ds(r0 + half * c + c - 1, 1), :])
            sbs = [s.astype(BF16) for s in sts]
            ws = [_dot(jnp.concatenate([w_ref[r, hcols(h)], q_ref[r, hcols(h)]], axis=0), sbs[h]) for h in heads]
            vn = [v_ref[r, hcols(h)] - ws[h][0:c] for h in heads]
            outs.append([ws[h][c:] for h in heads])
            sts = [sts[h] * egl[:, h:h + 1] + _dot_tn(k_ref[r, hcols(h)], vn[h].astype(BF16))
                   for h in heads]
            vns.append(vn)
        for h in heads:
            state_ref[h] = sts[h]
        for h in heads:
            vn = jnp.concatenate([vns[0][h], vns[1][h]], axis=0).astype(BF16)
            o = jnp.concatenate([outs[0][h], outs[1][h]], axis=0) + _dot(qk_ref[rows, hcols(h)], vn)
            o = o * lax.rsqrt(jnp.mean(o * o, axis=-1, keepdims=True) + EPS) * on_ref[...]
            ob_ref[rows, hcols(h)] = (o * _silu(z_ref[rows, hcols(h)].astype(F32))).astype(BF16)

    n_pairs = ts // (2 * c)
    wy_factors(0)
    for dc in range(n_pairs):
        if dc + 1 < n_pairs:
            wy_factors(dc + 1)
        recurrence(dc)
    o_ref[...] = x_ref[...] + _dot(ob_ref[...], wout_ref[...])


def gated_deltanet(qkv, z, gates, x, a_log, dt_bias, o_norm, w_out, batch, seq, ts):
    t, d = x.shape
    width = GDN_HEADS * GDN_HEAD_DIM
    ns = seq // ts

    def lane_pad(vec):
        return jnp.pad(vec, (0, LANES - vec.shape[0])).reshape(1, LANES)

    def row(w, j):
        return pl.BlockSpec((ts, w), lambda b, s: (b * ns + s, j))

    return pl.pallas_call(
        _gdn_kernel, grid=(batch, ns),
        in_specs=[row(3 * width, 0), row(width, 0), row(LANES, 0), row(LANES, 1), row(d, 0),
                  _resident((1, LANES)), _resident((1, LANES)), _resident((1, GDN_HEAD_DIM)),
                  _resident(w_out.shape)],
        out_specs=row(d, 0), out_shape=jax.ShapeDtypeStruct((t, d), F32),
        scratch_shapes=[pltpu.VMEM((ts, width), BF16), pltpu.VMEM((ts, width), BF16),
                        pltpu.VMEM((ts, width), F32),
                        pltpu.VMEM((ts, width), BF16), pltpu.VMEM((ts, width), BF16),
                        pltpu.VMEM((ts, LANES), F32), pltpu.VMEM((LANES, ts), F32),
                        pltpu.VMEM((ts, LANES), F32), pltpu.VMEM((ts, width), BF16),
                        pltpu.VMEM((GDN_HEADS, GDN_HEAD_DIM, GDN_HEAD_DIM), F32)],
        compiler_params=_params("parallel", "arbitrary", vmem_limit_bytes=GDN_VMEM_LIMIT_BYTES),
        name="gated_deltanet",
    )(qkv, z, gates, gates, x, lane_pad(a_log), lane_pad(dt_bias), o_norm.reshape(1, GDN_HEAD_DIM), w_out)


def _common_block(x, mem, xa_norm, mem_norm, xa_wq, xa_wkv, xa_wo, ffn_norm, ffn_w_up, ffn_conv,
                  ffn_w_down, final_gain, batch, seq, tm, mix0=None):
    (kv,) = norm_proj(mem, mem_norm, [xa_wkv.astype(BF16)], [BF16], tm=min(tm, mem.shape[0]))
    x = xattn(x, kv, xa_norm, xa_wq.astype(BF16), xa_wo.astype(BF16), batch, seq, tm, mix0)
    return conv_ffn(x, ffn_norm, ffn_w_up.astype(BF16), ffn_conv, ffn_w_down.astype(BF16), final_gain,
                    batch, seq, tm)


def kernel(x, mem, l0_mix_norm, l0_w_in, l0_ret_norm, l0_s5_lambda_re, l0_s5_lambda_im, l0_s5_b_re, l0_s5_b_im, l0_s5_c_re, l0_s5_c_im, l0_s5_d, l0_s5_log_dt, l0_s5_w_glu, l0_s5_b_glu, l0_w_out, l0_xa_norm, l0_mem_norm, l0_xa_wq, l0_xa_wkv, l0_xa_wo, l0_ffn_norm, l0_ffn_w_up, l0_ffn_conv, l0_ffn_w_down, l1_mix_norm, l1_w_in, l1_conv, l1_a_log, l1_dt_bias, l1_o_norm, l1_w_out, l1_xa_norm, l1_mem_norm, l1_xa_wq, l1_xa_wkv, l1_xa_wo, l1_ffn_norm, l1_ffn_w_up, l1_ffn_conv, l1_ffn_w_down, final_norm):
    batch, seq, d = x.shape
    tm = min(ROW_TILE, seq)
    xf = x.reshape(batch * seq, d)
    memf = mem.reshape(-1, d)
    ret_width = RET_HEADS * RET_HEAD_DIM

    o_ret, u3 = even_mixer_in(xf, l0_mix_norm, l0_w_in[:, :4 * ret_width].astype(BF16),
                              l0_w_in[:, 4 * ret_width:].T.astype(BF16), l0_ret_norm, batch, seq, tm)
    y3 = s5_ssm(u3, l0_s5_lambda_re, l0_s5_lambda_im, l0_s5_b_re, l0_s5_b_im, l0_s5_c_re, l0_s5_c_im,
                l0_s5_d, l0_s5_log_dt, batch)
    mix0 = (o_ret, y3, l0_s5_w_glu.astype(BF16), l0_s5_b_glu, l0_w_out.astype(BF16))
    xf = _common_block(xf, memf, l0_xa_norm, l0_mem_norm, l0_xa_wq, l0_xa_wkv, l0_xa_wo, l0_ffn_norm,
                       l0_ffn_w_up, l0_ffn_conv, l0_ffn_w_down, None, batch, seq, tm, mix0)

    gdn_width = GDN_HEADS * GDN_HEAD_DIM
    w_qkv = l1_w_in[:, :3 * gdn_width].astype(BF16)
    w_z = l1_w_in[:, 3 * gdn_width:4 * gdn_width].astype(BF16)
    w_b = jnp.pad(l1_w_in[:, 4 * gdn_width:4 * gdn_width + GDN_HEADS], ((0, 0), (0, LANES - GDN_HEADS)))
    w_a = jnp.pad(l1_w_in[:, 4 * gdn_width + GDN_HEADS:], ((0, 0), (0, LANES - GDN_HEADS)))
    w_gates = jnp.concatenate([w_b, w_a], axis=1).astype(BF16)
    qkv, z, gates = gdn_proj(xf, l1_mix_norm, w_qkv, w_z, w_gates, l1_conv, batch, seq, tm)
    xf = gated_deltanet(qkv, z, gates, xf, l1_a_log, l1_dt_bias, l1_o_norm, l1_w_out.astype(BF16),
                        batch, seq, tm)
    xf = _common_block(xf, memf, l1_xa_norm, l1_mem_norm, l1_xa_wq, l1_xa_wkv, l1_xa_wo, l1_ffn_norm,
                       l1_ffn_w_up, l1_ffn_conv, l1_ffn_w_down, final_norm, batch, seq, tm)
    return xf.reshape(batch, seq, d)
```

```python
import functools
import math

import jax
import jax.numpy as jnp
from jax import lax
from jax.experimental import pallas as pl
from jax.experimental.pallas import tpu as pltpu

F32 = jnp.float32
BF16 = jnp.bfloat16
EPS = 1e-6

V7X_VMEM_BYTES = 64 * 1024 * 1024
VMEM_LIMIT_BYTES = V7X_VMEM_BYTES - 8 * 1024 * 1024
LANES = 128
SUBLANES = 8

RET_HEADS = 4
RET_HEAD_DIM = 128
RET_CHUNK = 128
ROPE_BASE = 10000.0
S5_GROUP = 16
S5_STATE = 64
S5_CHUNK = 64
GDN_HEADS = 8
GDN_HEAD_DIM = 128
GDN_CONV = 4
GDN_CHUNK = 64
XA_HEADS = 4
FFN_CONV = 3
FFN_COLS = 256
ROW_TILE = 1024
GDN_ROW_TILE = 512

HIGHEST = lax.Precision.HIGHEST


def _params(*semantics):
    return pltpu.CompilerParams(dimension_semantics=semantics, vmem_limit_bytes=VMEM_LIMIT_BYTES)


def _resident(shape):
    nd = len(shape)
    return pl.BlockSpec(shape, lambda *_: (0,) * nd)


def _dot(a, b, precision=None):
    return jnp.dot(a, b, preferred_element_type=F32, precision=precision)


def _dot_nt(a, b):
    return lax.dot_general(a, b, (((1,), (1,)), ((), ())), preferred_element_type=F32)


def _dot_tn(a, b):
    return lax.dot_general(a, b, (((0,), (0,)), ((), ())), preferred_element_type=F32)


def _rms(x, g):
    return x * lax.rsqrt(jnp.mean(x * x, axis=-1, keepdims=True) + EPS) * g


def _silu(x):
    h = 0.5 * x
    return h + h * jnp.tanh(h)


def _norm_proj_kernel(x_ref, *refs):
    n = len(refs) // 3
    x = x_ref[...]
    xhat = x * lax.rsqrt(jnp.mean(x * x, axis=-1, keepdims=True) + EPS)
    for g_ref, w_ref, o_ref in zip(refs[:n], refs[n:2 * n], refs[2 * n:]):
        o_ref[...] = _dot((xhat * g_ref[...]).astype(BF16), w_ref[...]).astype(o_ref.dtype)


def norm_proj(x, gains, weights, out_dtypes, tm):
    t, d = x.shape
    in_specs = [pl.BlockSpec((tm, d), lambda i: (i, 0))] + [_resident((1, d)) for _ in gains]
    in_specs += [_resident(w.shape) for w in weights]
    out_specs = [pl.BlockSpec((tm, w.shape[1]), lambda i: (i, 0)) for w in weights]
    out_shape = [jax.ShapeDtypeStruct((t, w.shape[1]), dt) for w, dt in zip(weights, out_dtypes)]
    return pl.pallas_call(
        _norm_proj_kernel, grid=(t // tm,), in_specs=in_specs, out_specs=out_specs,
        out_shape=out_shape, compiler_params=_params("parallel"), name="norm_proj",
    )(x, *[g.reshape(1, d) for g in gains], *weights)


def _even_mixer_kernel(x_ref, g_ref, w_ref, wut_ref, cos_ref, sin_ref, intra_ref, qdec_ref, kdec_ref,
                       cdec_ref, rn_ref, o_ref, u3_ref, proj_ref, state_ref):
    c = RET_CHUNK
    width = RET_HEADS * RET_HEAD_DIM

    @pl.when(pl.program_id(1) == 0)
    def _():
        state_ref[...] = jnp.zeros_like(state_ref)

    xn = _rms(x_ref[...], g_ref[...]).astype(BF16)
    ut = _dot_nt(wut_ref[...], xn)
    for k in range(u3_ref.shape[0]):
        u3_ref[k] = ut[:, k * LANES:(k + 1) * LANES]
    proj_ref[...] = _dot(xn, w_ref[...]).astype(proj_ref.dtype)

    scale = RET_HEAD_DIM ** -0.5
    for ci in range(x_ref.shape[0] // c):
        rows = pl.ds(ci * c, c)
        cos = cos_ref[rows, :]
        sin = sin_ref[rows, :]
        for h in range(RET_HEADS):
            cols = pl.ds(h * RET_HEAD_DIM, RET_HEAD_DIM)

            def part(j):
                return proj_ref[rows, pl.ds(j * width + h * RET_HEAD_DIM, RET_HEAD_DIM)]

            qh = part(0).astype(F32)
            kh = part(1).astype(F32)
            vh = part(2)
            qr = qh * cos + pltpu.roll(qh, RET_HEAD_DIM // 2, 1) * sin
            kr = (kh * cos + pltpu.roll(kh, RET_HEAD_DIM // 2, 1) * sin) * scale
            scores = _dot_nt(qr.astype(BF16), kr.astype(BF16)) * intra_ref[h]
            inner = _dot(scores.astype(BF16), vh)
            st = state_ref[h]
            cross = _dot((qr * qdec_ref[:, cols]).astype(BF16), st.astype(BF16))
            kv = _dot_tn((kr * kdec_ref[:, cols]).astype(BF16), vh)
            state_ref[h] = st * cdec_ref[:, cols] + kv
            o = inner + cross
            o = o * lax.rsqrt(jnp.mean(o * o, axis=-1, keepdims=True) + EPS)
            o_ref[rows, cols] = (o * rn_ref[:, cols] * _silu(part(3).astype(F32))).astype(o_ref.dtype)


def even_mixer_in(x, gain, w, w_u_t, ret_norm, batch, seq, ts):
    t, d = x.shape
    nu = w_u_t.shape[0]
    width = RET_HEADS * RET_HEAD_DIM
    c = RET_CHUNK
    half = RET_HEAD_DIM // 2
    ns = seq // ts
    inv = jnp.exp(-math.log(ROPE_BASE) * jnp.arange(half, dtype=F32) / half)
    ang = jnp.arange(seq, dtype=F32)[:, None] * inv[None, :]
    cos = jnp.concatenate([jnp.cos(ang), jnp.cos(ang)], axis=-1)
    sin = jnp.concatenate([-jnp.sin(ang), jnp.sin(ang)], axis=-1)
    log_gamma = jnp.log1p(-jnp.exp2(-5.0 - jnp.arange(RET_HEADS, dtype=F32)))
    idx = jnp.arange(c, dtype=F32)
    diff = idx[:, None] - idx[None, :]
    causal = diff >= 0
    intra = jnp.where(causal, jnp.exp(log_gamma[:, None, None] * jnp.where(causal, diff, 0.0)), 0.0)

    def per_head_cols(tab):
        return jnp.repeat(tab.T, RET_HEAD_DIM, axis=1)

    qdec = per_head_cols(jnp.exp(log_gamma[:, None] * (idx + 1)))
    kdec = per_head_cols(jnp.exp(log_gamma[:, None] * (c - 1 - idx)))
    cdec = per_head_cols(jnp.exp(log_gamma * c)[:, None])

    in_specs = [pl.BlockSpec((ts, d), lambda b, s: (b * ns + s, 0)), _resident((1, d)), _resident(w.shape),
                _resident(w_u_t.shape),
                pl.BlockSpec((ts, RET_HEAD_DIM), lambda b, s: (s, 0)),
                pl.BlockSpec((ts, RET_HEAD_DIM), lambda b, s: (s, 0)),
                _resident(intra.shape), _resident(qdec.shape), _resident(kdec.shape),
                _resident(cdec.shape), _resident((1, width))]
    return pl.pallas_call(
        _even_mixer_kernel, grid=(batch, ns), in_specs=in_specs,
        out_specs=[pl.BlockSpec((ts, width), lambda b, s: (b * ns + s, 0)),
                   pl.BlockSpec((ts // LANES, nu, LANES), lambda b, s: (b * ns + s, 0, 0))],
        out_shape=[jax.ShapeDtypeStruct((t, width), BF16), jax.ShapeDtypeStruct((t // LANES, nu, LANES), F32)],
        scratch_shapes=[pltpu.VMEM((ts, w.shape[1]), BF16),
                        pltpu.VMEM((RET_HEADS, RET_HEAD_DIM, RET_HEAD_DIM), F32)],
        compiler_params=_params("parallel", "arbitrary"), name="even_mixer_in",
    )(x, gain.reshape(1, d), w, w_u_t, cos, sin, intra, qdec, kdec, cdec, ret_norm.reshape(1, width))


def _s5_prep_kernel(lr_ref, li_ref, ldt_ref, lrc_ref, lic_ref, ldtc_ref, br_ref, bi_ref, cr_ref, ci_ref,
                    ktoe_ref, bdr_ref, bdi_ref, cdr_ref, cdi_ref, al_ref, *, chunk):
    hg = S5_GROUP
    n = chunk * hg
    lr, li = lr_ref[...], li_ref[...]
    dt = jnp.exp(ldt_ref[...])
    mag = jnp.exp(lr * dt)
    a_re = mag * jnp.cos(li * dt)
    a_im = mag * jnp.sin(li * dt)
    den = lr * lr + li * li
    z_re = ((a_re - 1.0) * lr + a_im * li) / den
    z_im = (a_im * lr - (a_re - 1.0) * li) / den

    def zoh(br, bi):
        return z_re * br - z_im * bi, z_re * bi + z_im * br

    e = (chunk - 1.0) - lax.broadcasted_iota(jnp.int32, (chunk, S5_STATE), 0).astype(F32)
    pm = jnp.exp(lr * dt * e)
    pr = jnp.concatenate([pm * jnp.cos(li * dt * e)] * hg, axis=0)
    pi = jnp.concatenate([pm * jnp.sin(li * dt * e)] * hg, axis=0)
    b16_re, b16_im = zoh(br_ref[...], bi_ref[...])

    def over_positions(b):
        return jnp.broadcast_to(b[:, None, :], (hg, chunk, S5_STATE)).reshape(n, S5_STATE)

    bb_re, bb_im = over_positions(b16_re), over_positions(b16_im)
    bdr_ref[...] = (bb_re * pr - bb_im * pi).astype(bdr_ref.dtype)
    bdi_ref[...] = (bb_re * pi + bb_im * pr).astype(bdi_ref.dtype)
    lm = jnp.exp(lr * dt * chunk)
    al_ref[0:1, :] = lm * jnp.cos(li * dt * chunk)
    al_ref[1:2, :] = lm * jnp.sin(li * dt * chunk)

    lrc, lic = lrc_ref[...], lic_ref[...]
    dtc = jnp.exp(ldtc_ref[...])
    tau = lax.broadcasted_iota(jnp.int32, (S5_STATE, chunk), 1).astype(F32)
    vm = jnp.exp(lrc * dtc * tau)
    vc = jnp.concatenate([vm * jnp.cos(lic * dtc * tau)] * hg, axis=1)
    vs = jnp.concatenate([vm * jnp.sin(lic * dtc * tau)] * hg, axis=1)

    def over_lanes(cm):
        return jnp.concatenate([jnp.broadcast_to(cm[:, ho:ho + 1], (S5_STATE, chunk)) for ho in range(hg)], axis=1)

    crr, cri = over_lanes(cr_ref[...]), over_lanes(ci_ref[...])
    v_re = vc * crr - vs * cri
    v_im = vc * cri + vs * crr
    magc = jnp.exp(lrc * dtc)
    ac_re = magc * jnp.cos(lic * dtc)
    ac_im = magc * jnp.sin(lic * dtc)
    cdr_ref[...] = (v_re * ac_re - v_im * ac_im).astype(cdr_ref.dtype)
    cdi_ref[...] = (-(v_re * ac_im + v_im * ac_re)).astype(cdi_ref.dtype)

    p = _dot(b16_re, v_re, HIGHEST) - _dot(b16_im, v_im, HIGHEST)
    row = lax.broadcasted_iota(jnp.int32, (chunk, n), 0)
    keep = lax.broadcasted_iota(jnp.int32, (chunk, n), 1) % chunk >= row
    for hi in range(hg):
        blk = pltpu.roll(jnp.broadcast_to(p[hi:hi + 1, :], (chunk, n)), 0, 1, stride=1, stride_axis=0)
        ktoe_ref[hi * chunk:(hi + 1) * chunk, :] = jnp.where(keep, blk, 0.0).astype(ktoe_ref.dtype)


def _s5_main_kernel(u_ref, ktoe_ref, bdr_ref, bdi_ref, cdr_ref, cdi_ref, al_ref, d_ref, y_ref,
                    sre_ref, sim_ref, *, batch):
    r2 = u_ref.shape[0]
    per_seq = r2 // batch
    c = S5_CHUNK
    pieces = [u_ref[:, hi, :] for hi in range(S5_GROUP)]
    lhs = jnp.concatenate([jnp.concatenate([p[:, 0:c] for p in pieces], axis=1),
                           jnp.concatenate([p[:, c:] for p in pieces], axis=1)], axis=0).astype(BF16)
    sre_ref[...] = _dot(lhs, bdr_ref[...])
    sim_ref[...] = _dot(lhs, bdi_ref[...])
    a_re = al_ref[0:1, :]
    a_im = al_ref[1:2, :]
    s_re = jnp.zeros((batch, S5_STATE), F32)
    s_im = jnp.zeros((batch, S5_STATE), F32)
    for i in range(per_seq):
        for base in (0, r2):
            rows = pl.ds(base + i, batch, stride=per_seq)
            e_re, e_im = sre_ref[rows, :], sim_ref[rows, :]
            sre_ref[rows, :] = s_re
            sim_ref[rows, :] = s_im
            s_re, s_im = a_re * s_re - a_im * s_im + e_re, a_re * s_im + a_im * s_re + e_im
    y = _dot(lhs, ktoe_ref[...])
    y += _dot(sre_ref[...].astype(BF16), cdr_ref[...])
    y += _dot(sim_ref[...].astype(BF16), cdi_ref[...])
    for ho in range(S5_GROUP):
        cols = slice(ho * c, (ho + 1) * c)
        y_ref[:, ho, :] = (jnp.concatenate([y[0:r2, cols], y[r2:, cols]], axis=1)
                           + d_ref[ho:ho + 1, :] * pieces[ho])


def s5_ssm(u3, lam_re, lam_im, b_re, b_im, c_re, c_im, d, log_dt, batch):
    r2, width, lanes = u3.shape
    g, hg, p, chunk = width // S5_GROUP, S5_GROUP, S5_STATE, S5_CHUNK
    assert lanes == LANES == 2 * chunk and r2 % batch == 0
    n = chunk * hg

    def grp(shape):
        return pl.BlockSpec((None,) + shape, lambda i: (i, 0, 0))

    ldt = jnp.broadcast_to(log_dt[:, None, None], (g, 1, p))
    prep_in = [lam_re[:, None, :], lam_im[:, None, :], ldt,
               lam_re[:, :, None], lam_im[:, :, None], jnp.swapaxes(ldt, 1, 2),
               b_re, b_im, c_re, c_im]
    prep_specs = [grp((1, p))] * 3 + [grp((p, 1))] * 3 + [grp((hg, p))] * 2 + [grp((p, hg))] * 2
    ktoe, bdr, bdi, cdr, cdi, al = pl.pallas_call(
        functools.partial(_s5_prep_kernel, chunk=chunk),
        grid=(g,), in_specs=prep_specs,
        out_specs=[grp((n, n)), grp((n, p)), grp((n, p)), grp((p, n)), grp((p, n)), grp((2, p))],
        out_shape=[jax.ShapeDtypeStruct((g, n, n), BF16),
                   jax.ShapeDtypeStruct((g, n, p), BF16), jax.ShapeDtypeStruct((g, n, p), BF16),
                   jax.ShapeDtypeStruct((g, p, n), BF16), jax.ShapeDtypeStruct((g, p, n), BF16),
                   jax.ShapeDtypeStruct((g, 2, p), F32)],
        compiler_params=_params("parallel"), name="s5_prep",
    )(*prep_in)

    channels = pl.BlockSpec((r2, hg, LANES), lambda i: (0, i, 0))
    dvec = jnp.broadcast_to(d[:, :, None], (g, hg, LANES))
    return pl.pallas_call(
        functools.partial(_s5_main_kernel, batch=batch),
        grid=(g,),
        in_specs=[channels, grp((n, n)), grp((n, p)), grp((n, p)), grp((p, n)), grp((p, n)),
                  grp((2, p)), grp((hg, LANES))],
        out_specs=channels,
        out_shape=jax.ShapeDtypeStruct(u3.shape, F32),
        scratch_shapes=[pltpu.VMEM((2 * r2, p), F32), pltpu.VMEM((2 * r2, p), F32)],
        compiler_params=_params("parallel"), name="s5_main",
    )(u3, ktoe, bdr, bdi, cdr, cdi, al, dvec)


def _mix0_rows(o_ref, y3_ref, x, wglu_ref, bglu_ref, wout_ref):
    y = jnp.concatenate([y3_ref[k].T for k in range(y3_ref.shape[0])], axis=0)
    y = 0.5 * y * (1.0 + jnp.tanh(math.sqrt(2.0 / math.pi) * (y + 0.044715 * (y * y * y))))
    y = y * jax.nn.sigmoid(_dot(y.astype(BF16), wglu_ref[...]) + bglu_ref[...])
    merged = jnp.concatenate([o_ref[...], y.astype(BF16)], axis=-1)
    return x + _dot(merged, wout_ref[...])


def _xattn_rows(x, g_ref, wq_ref, kv_ref, wo_ref):
    d = x.shape[1]
    dh = d // XA_HEADS
    xn = _rms(x, g_ref[...]).astype(BF16)
    q = _dot(xn, wq_ref[...]).astype(BF16)
    heads = []
    for h in range(XA_HEADS):
        kh = kv_ref[:, pl.ds(h * dh, dh)]
        vh = kv_ref[:, pl.ds(d + h * dh, dh)]
        s = _dot_nt(q[:, h * dh:(h + 1) * dh], kh) * (dh ** -0.5)
        e = jnp.exp(s - jnp.max(s, axis=-1, keepdims=True))
        p = e * (1.0 / jnp.sum(e, axis=-1, keepdims=True))
        heads.append(_dot(p.astype(BF16), vh).astype(BF16))
    return x + _dot(jnp.concatenate(heads, axis=-1), wo_ref[...])


def _xattn_kernel(x_ref, g_ref, wq_ref, kv_ref, wo_ref, o_ref):
    o_ref[...] = _xattn_rows(x_ref[...], g_ref, wq_ref, kv_ref, wo_ref)


def _mix0_xattn_kernel(o_ref, y3_ref, x_ref, wglu_ref, bglu_ref, wout_ref, g_ref, wq_ref, kv_ref, wo_ref,
                       out_ref):
    x = _mix0_rows(o_ref, y3_ref, x_ref[...], wglu_ref, bglu_ref, wout_ref)
    out_ref[...] = _xattn_rows(x, g_ref, wq_ref, kv_ref, wo_ref)


def xattn(x, kv, gain, wq, wo, batch, seq, tm, mix0=None):
    t, d = x.shape
    n_mem = kv.shape[0] // batch
    ns = seq // tm
    row = pl.BlockSpec((tm, d), lambda b, s: (b * ns + s, 0))
    attn_specs = [_resident((1, d)), _resident(wq.shape), pl.BlockSpec((n_mem, 2 * d), lambda b, s: (b, 0)),
                  _resident(wo.shape)]
    attn_args = (gain.reshape(1, d), wq, kv, wo)
    if mix0 is None:
        body, in_specs, args = _xattn_kernel, [row] + attn_specs, (x,) + attn_args
    else:
        o, y3, w_glu, b_glu, w_out = mix0
        half = o.shape[1]
        body = _mix0_xattn_kernel
        in_specs = [pl.BlockSpec((tm, half), lambda b, s: (b * ns + s, 0)),
                    pl.BlockSpec((tm // LANES, half, LANES), lambda b, s: (b * ns + s, 0, 0)), row,
                    _resident(w_glu.shape), _resident((1, half)), _resident(w_out.shape)] + attn_specs
        args = (o, y3, x, w_glu, b_glu.reshape(1, half), w_out) + attn_args
    return pl.pallas_call(
        body, grid=(batch, ns), in_specs=in_specs, out_specs=row,
        out_shape=jax.ShapeDtypeStruct((t, d), F32),
        compiler_params=_params("parallel", "parallel"), name="xattn",
    )(*args)


def _ffn_kernel(x_ref, g_ref, wup_ref, cw_ref, wd_ref, fg_ref, o_ref, xn_ref, tail_ref, act_ref,
                *, final_norm):
    tm = x_ref.shape[0]
    f = wd_ref.shape[0]
    fc = FFN_COLS
    pad = SUBLANES

    @pl.when(pl.program_id(1) == 0)
    def _():
        tail_ref[...] = jnp.zeros_like(tail_ref)

    xn_ref[...] = _rms(x_ref[...], g_ref[...]).astype(BF16)

    def proj_conv(col0):
        cols = pl.ds(col0, fc)
        cur = _dot(xn_ref[...], wup_ref[:, cols])
        z = jnp.concatenate([tail_ref[:, cols], cur], axis=0)
        tail_ref[:, cols] = cur[tm - pad:, :]
        y = cw_ref[0:1, cols] * z
        y = cw_ref[1:2, cols] * z + pltpu.roll(y, 1, 0)
        y = cw_ref[2:3, cols] * z + pltpu.roll(y, 1, 0)
        return y[pad:, :]

    for cb in range(f // fc):
        up = proj_conv(cb * fc)
        gate = proj_conv(f + cb * fc)
        act_ref[:, pl.ds(cb * fc, fc)] = (_silu(gate) * up).astype(BF16)
    y = x_ref[...] + _dot(act_ref[...], wd_ref[...])
    if final_norm:
        y = _rms(y, fg_ref[...])
    o_ref[...] = y


def conv_ffn(x, gain, w_up, conv_w, w_down, final_gain, batch, seq, tm):
    t, d = x.shape
    f = w_down.shape[0]
    assert FFN_CONV == conv_w.shape[0] == 3 and f % FFN_COLS == 0
    ns = seq // tm
    final_norm = final_gain is not None
    fg = (final_gain if final_norm else jnp.ones((d,), F32)).reshape(1, d)
    row = pl.BlockSpec((tm, d), lambda b, s: (b * ns + s, 0))
    return pl.pallas_call(
        functools.partial(_ffn_kernel, final_norm=final_norm), grid=(batch, ns),
        in_specs=[row, _resident((1, d)), _resident(w_up.shape), _resident(conv_w.shape),
                  _resident(w_down.shape), _resident((1, d))],
        out_specs=row, out_shape=jax.ShapeDtypeStruct((t, d), F32),
        scratch_shapes=[pltpu.VMEM((tm, d), BF16), pltpu.VMEM((SUBLANES, 2 * f), F32),
                        pltpu.VMEM((tm, f), BF16)],
        compiler_params=_params("parallel", "arbitrary"), name="conv_ffn",
    )(x, gain.reshape(1, d), w_up, conv_w, w_down, fg)


def _unit_lower_inverses(a_list, eye):
    n = eye.shape[0]
    ms = [eye - a for a in a_list]
    ps = [(-a).astype(BF16) for a in a_list]
    ps = [_dot(p, p).astype(BF16) for p in ps]
    for _ in range(4):
        both = [_dot(jnp.concatenate([m.astype(BF16), p], axis=0), p) for m, p in zip(ms, ps)]
        ms = [m + b[0:n] for m, b in zip(ms, both)]
        ps = [b[n:].astype(BF16) for b in both]
    return [m + _dot(m.astype(BF16), p) for m, p in zip(ms, ps)]


def _gdn_proj_kernel(x_ref, g_ref, wqkv_ref, wz_ref, wg_ref, cw_ref, qkv_ref, z_ref, gates_ref,
                     xn_ref, tail_ref):
    tm = x_ref.shape[0]
    width = GDN_HEADS * GDN_HEAD_DIM
    dh = GDN_HEAD_DIM
    cb_cols = 2 * dh
    pad = SUBLANES

    @pl.when(pl.program_id(1) == 0)
    def _():
        tail_ref[...] = jnp.zeros_like(tail_ref)

    xn_ref[...] = _rms(x_ref[...], g_ref[...]).astype(BF16)
    for cb in range(3 * width // cb_cols):
        cols = pl.ds(cb * cb_cols, cb_cols)
        cur = _dot(xn_ref[...], wqkv_ref[:, cols])
        zz = jnp.concatenate([tail_ref[:, cols], cur], axis=0)
        tail_ref[:, cols] = cur[tm - pad:, :]
        zs = pltpu.roll(zz, 1, 0)
        near = cw_ref[3:4, cols] * zz + cw_ref[2:3, cols] * zs
        far = cw_ref[1:2, cols] * zz + cw_ref[0:1, cols] * zs
        y = _silu((near + pltpu.roll(far, 2, 0))[pad:, :])
        if cb * cb_cols < 2 * width:
            scale = dh ** -0.5 if cb * cb_cols < width else 1.0
            parts = []
            for j in range(cb_cols // dh):
                yh = y[:, j * dh:(j + 1) * dh]
                parts.append(yh * (lax.rsqrt(jnp.sum(yh * yh, axis=-1, keepdims=True) + EPS) * scale))
            y = jnp.concatenate(parts, axis=1)
        qkv_ref[:, cols] = y.astype(qkv_ref.dtype)
        if cb % 3 == 2:
            zc = pl.ds((cb // 3) * cb_cols, cb_cols)
            z_ref[:, zc] = _dot(xn_ref[...], wz_ref[:, zc]).astype(z_ref.dtype)
        if cb == 0:
            gates_ref[...] = _dot(xn_ref[...], wg_ref[...])


def gdn_proj(x, gain, w_qkv, w_z, w_gates, conv_w, batch, seq, tm):
    t, d = x.shape
    ns = seq // tm
    assert conv_w.shape[0] == GDN_CONV

    def row(w):
        return pl.BlockSpec((tm, w), lambda b, s: (b * ns + s, 0))

    widths = [w_qkv.shape[1], w_z.shape[1], w_gates.shape[1]]
    return pl.pallas_call(
        _gdn_proj_kernel, grid=(batch, ns),
        in_specs=[row(d), _resident((1, d)), _resident(w_qkv.shape), _resident(w_z.shape),
                  _resident(w_gates.shape), _resident(conv_w.shape)],
        out_specs=[row(w) for w in widths],
        out_shape=[jax.ShapeDtypeStruct((t, w), dt) for w, dt in zip(widths, (BF16, BF16, F32))],
        scratch_shapes=[pltpu.VMEM((tm, d), BF16), pltpu.VMEM((SUBLANES, w_qkv.shape[1]), F32)],
        compiler_params=_params("parallel", "arbitrary"), name="gdn_proj",
    )(x, gain.reshape(1, d), w_qkv, w_z, w_gates, conv_w)


def _gdn_kernel(qkv_ref, z_ref, b_ref, a_ref, x_ref, alog_ref, dtb_ref, on_ref, wout_ref, o_ref,
                q_ref, k_ref, v_ref, w_ref, qk_ref, gc_ref, gct_ref, beta_ref, ob_ref, state_ref):
    ts = qkv_ref.shape[0]
    width = GDN_HEADS * GDN_HEAD_DIM
    dh = GDN_HEAD_DIM
    c = GDN_CHUNK
    heads = range(GDN_HEADS)

    @pl.when(pl.program_id(1) == 0)
    def _():
        state_ref[...] = jnp.zeros_like(state_ref)

    beta_ref[...] = jax.nn.sigmoid(b_ref[...])
    ag = a_ref[...] + dtb_ref[...]
    softplus = jnp.maximum(ag, 0.0) + jnp.log1p(jnp.exp(-jnp.abs(ag)))
    gc = -(jnp.exp(alog_ref[...]) * softplus)
    pos = lax.broadcasted_iota(jnp.int32, (ts, LANES), 0) % c
    shift = 1
    while shift < c:
        gc = gc + jnp.where(pos >= shift, pltpu.roll(gc, shift, 0), 0.0)
        shift *= 2
    gc_ref[...] = gc
    gct_ref[...] = gc.T

    ri = lax.broadcasted_iota(jnp.int32, (2 * c, 2 * c), 0)
    ci = lax.broadcasted_iota(jnp.int32, (2 * c, 2 * c), 1)
    same = (ri // c) == (ci // c)
    incl = same & (ri >= ci)
    strict = same & (ri > ci)
    eye = jnp.where(ri == ci, 1.0, 0.0).astype(F32)

    def hcols(h):
        return pl.ds(h * dh, dh)

    def wy_factors(dc):
        r0 = dc * 2 * c
        rows = pl.ds(r0, 2 * c)
        gcd = gc_ref[rows, :]
        gctd = gct_ref[:, rows]
        bet = beta_ref[rows, :]
        qbf = [qkv_ref[rows, hcols(h)] for h in heads]
        kbf = [qkv_ref[rows, hcols(GDN_HEADS + h)] for h in heads]
        qs = [q.astype(F32) for q in qbf]
        ks = [k.astype(F32) for k in kbf]
        vs = [qkv_ref[rows, hcols(2 * GDN_HEADS + h)].astype(F32) for h in heads]
        gcols = [gcd[:, h:h + 1] for h in heads]
        bcols = [bet[:, h:h + 1] for h in heads]
        decays = [jnp.where(incl, jnp.exp(jnp.where(incl, gcols[h] - gctd[h:h + 1, :], 0.0)), 0.0)
                  for h in heads]
        kbs = [ks[h] * bcols[h] for h in heads]
        both = [_dot_nt(jnp.concatenate([kbs[h].astype(BF16), qbf[h]], axis=0), kbf[h]) for h in heads]
        a_mats = [jnp.where(strict, both[h][0:2 * c] * decays[h], 0.0) for h in heads]
        for h in heads:
            qk_ref[rows, hcols(h)] = jnp.where(incl, both[h][2 * c:] * decays[h], 0.0).astype(BF16)
        t_mats = [t.astype(BF16) for t in _unit_lower_inverses(a_mats, eye)]
        egs = [jnp.exp(g) for g in gcols]
        for h in heads:
            w_ref[rows, hcols(h)] = _dot(t_mats[h], (kbs[h] * egs[h]).astype(BF16)).astype(BF16)
        for h in heads:
            v_ref[rows, hcols(h)] = _dot(t_mats[h], (vs[h] * bcols[h]).astype(BF16))
        for h in heads:
            q_ref[rows, hcols(h)] = (qs[h] * egs[h]).astype(BF16)
            gl0 = gcd[c - 1:c, h:h + 1]
            gl1 = gcd[2 * c - 1:2 * c, h:h + 1]
            glcol = jnp.concatenate([jnp.broadcast_to(gl0, (c, 1)), jnp.broadcast_to(gl1, (c, 1))], axis=0)
            k_ref[rows, hcols(h)] = (ks[h] * jnp.exp(glcol - gcols[h])).astype(BF16)

    def recurrence(dc):
        r0 = dc * 2 * c
        rows = pl.ds(r0, 2 * c)
        sts = [state_ref[h] for h in heads]
        vns, outs = [], []
        for half in range(2):
            r = pl.ds(r0 + half * c, c)
            egl = jnp.exp(gc_ref[pl.ds(r0 + half * c + c - 1, 1), :])
            sbs = [s.astype(BF16) for s in sts]
            ws = [_dot(jnp.concatenate([w_ref[r, hcols(h)], q_ref[r, hcols(h)]], axis=0), sbs[h]) for h in heads]
            vn = [v_ref[r, hcols(h)] - ws[h][0:c] for h in heads]
            outs.append([ws[h][c:] for h in heads])
            sts = [sts[h] * egl[:, h:h + 1] + _dot_tn(k_ref[r, hcols(h)], vn[h].astype(BF16))
                   for h in heads]
            vns.append(vn)
        for h in heads:
            state_ref[h] = sts[h]
        for h in heads:
            vn = jnp.concatenate([vns[0][h], vns[1][h]], axis=0).astype(BF16)
            o = jnp.concatenate([outs[0][h], outs[1][h]], axis=0) + _dot(qk_ref[rows, hcols(h)], vn)
            o = o * lax.rsqrt(jnp.mean(o * o, axis=-1, keepdims=True) + EPS) * on_ref[...]
            ob_ref[rows, hcols(h)] = (o * _silu(z_ref[rows, hcols(h)].astype(F32))).astype(BF16)

    n_pairs = ts // (2 * c)
    wy_factors(0)
    for dc in range(n_pairs):
        if dc + 1 < n_pairs:
            wy_factors(dc + 1)
        recurrence(dc)
    o_ref[...] = x_ref[...] + _dot(ob_ref[...], wout_ref[...])


def gated_deltanet(qkv, z, gates, x, a_log, dt_bias, o_norm, w_out, batch, seq, ts):
    t, d = x.shape
    width = GDN_HEADS * GDN_HEAD_DIM
    ns = seq // ts

    def lane_pad(vec):
        return jnp.pad(vec, (0, LANES - vec.shape[0])).reshape(1, LANES)

    def row(w, j):
        return pl.BlockSpec((ts, w), lambda b, s: (b * ns + s, j))

    return pl.pallas_call(
        _gdn_kernel, grid=(batch, ns),
        in_specs=[row(3 * width, 0), row(width, 0), row(LANES, 0), row(LANES, 1), row(d, 0),
                  _resident((1, LANES)), _resident((1, LANES)), _resident((1, GDN_HEAD_DIM)),
                  _resident(w_out.shape)],
        out_specs=row(d, 0), out_shape=jax.ShapeDtypeStruct((t, d), F32),
        scratch_shapes=[pltpu.VMEM((ts, width), BF16), pltpu.VMEM((ts, width), BF16),
                        pltpu.VMEM((ts, width), F32),
                        pltpu.VMEM((ts, width), BF16), pltpu.VMEM((ts, width), BF16),
                        pltpu.VMEM((ts, LANES), F32), pltpu.VMEM((LANES, ts), F32),
                        pltpu.VMEM((ts, LANES), F32), pltpu.VMEM((ts, width), BF16),
                        pltpu.VMEM((GDN_HEADS, GDN_HEAD_DIM, GDN_HEAD_DIM), F32)],
        compiler_params=_params("parallel", "arbitrary"), name="gated_deltanet",
    )(qkv, z, gates, gates, x, lane_pad(a_log), lane_pad(dt_bias), o_norm.reshape(1, GDN_HEAD_DIM), w_out)


def _common_block(x, kv, xa_norm, xa_wq, xa_wo, ffn_norm, ffn_w_up, ffn_conv,
                  ffn_w_down, final_gain, batch, seq, tm, mix0=None):
    x = xattn(x, kv, xa_norm, xa_wq.astype(BF16), xa_wo.astype(BF16), batch, seq, tm, mix0)
    return conv_ffn(x, ffn_norm, ffn_w_up.astype(BF16), ffn_conv, ffn_w_down.astype(BF16), final_gain,
                    batch, seq, tm)


def kernel(x, mem, l0_mix_norm, l0_w_in, l0_ret_norm, l0_s5_lambda_re, l0_s5_lambda_im, l0_s5_b_re, l0_s5_b_im, l0_s5_c_re, l0_s5_c_im, l0_s5_d, l0_s5_log_dt, l0_s5_w_glu, l0_s5_b_glu, l0_w_out, l0_xa_norm, l0_mem_norm, l0_xa_wq, l0_xa_wkv, l0_xa_wo, l0_ffn_norm, l0_ffn_w_up, l0_ffn_conv, l0_ffn_w_down, l1_mix_norm, l1_w_in, l1_conv, l1_a_log, l1_dt_bias, l1_o_norm, l1_w_out, l1_xa_norm, l1_mem_norm, l1_xa_wq, l1_xa_wkv, l1_xa_wo, l1_ffn_norm, l1_ffn_w_up, l1_ffn_conv, l1_ffn_w_down, final_norm):
    batch, seq, d = x.shape
    tm = min(ROW_TILE, seq)
    xf = x.reshape(batch * seq, d)
    memf = mem.reshape(-1, d)
    ret_width = RET_HEADS * RET_HEAD_DIM

    o_ret, u3 = even_mixer_in(xf, l0_mix_norm, l0_w_in[:, :4 * ret_width].astype(BF16),
                              l0_w_in[:, 4 * ret_width:].T.astype(BF16), l0_ret_norm, batch, seq, tm)
    y3 = s5_ssm(u3, l0_s5_lambda_re, l0_s5_lambda_im, l0_s5_b_re, l0_s5_b_im, l0_s5_c_re, l0_s5_c_im,
                l0_s5_d, l0_s5_log_dt, batch)
    mix0 = (o_ret, y3, l0_s5_w_glu.astype(BF16), l0_s5_b_glu, l0_w_out.astype(BF16))
    kv0, kv1 = norm_proj(memf, [l0_mem_norm, l1_mem_norm], [l0_xa_wkv.astype(BF16), l1_xa_wkv.astype(BF16)],
                         [BF16, BF16], tm=min(tm, memf.shape[0]))
    xf = _common_block(xf, kv0, l0_xa_norm, l0_xa_wq, l0_xa_wo, l0_ffn_norm,
                       l0_ffn_w_up, l0_ffn_conv, l0_ffn_w_down, None, batch, seq, tm, mix0)

    gdn_width = GDN_HEADS * GDN_HEAD_DIM
    w_qkv = l1_w_in[:, :3 * gdn_width].astype(BF16)
    w_z = l1_w_in[:, 3 * gdn_width:4 * gdn_width].astype(BF16)
    w_b = jnp.pad(l1_w_in[:, 4 * gdn_width:4 * gdn_width + GDN_HEADS], ((0, 0), (0, LANES - GDN_HEADS)))
    w_a = jnp.pad(l1_w_in[:, 4 * gdn_width + GDN_HEADS:], ((0, 0), (0, LANES - GDN_HEADS)))
    w_gates = jnp.concatenate([w_b, w_a], axis=1).astype(BF16)
    qkv, z, gates = gdn_proj(xf, l1_mix_norm, w_qkv, w_z, w_gates, l1_conv, batch, seq, tm)
    xf = gated_deltanet(qkv, z, gates, xf, l1_a_log, l1_dt_bias, l1_o_norm, l1_w_out.astype(BF16),
                        batch, seq, min(GDN_ROW_TILE, tm))
    xf = _common_block(xf, kv1, l1_xa_norm, l1_xa_wq, l1_xa_wo, l1_ffn_norm,
                       l1_ffn_w_up, l1_ffn_conv, l1_ffn_w_down, final_norm, batch, seq, tm)
    return xf.reshape(batch, seq, d)
```
